```python
import math
import jax, jax.numpy as jnp
from jax import lax
import numpy as np

D_MODEL = 2048
BATCH = 1
SEQ = 8192
DEPTH = 4
DEC_BATCH = 16
DEC_SEQ = 64
PAST_LEN = 2048

CHUNK = 64
CONV_A_WIDTH = 31
C_A = D_MODEL
SSM_EXPAND = 2
D_INNER = SSM_EXPAND * D_MODEL
HEAD_DIM = 64
N_HEADS = D_INNER // HEAD_DIM
N_GROUPS = 8
HEADS_PER_GROUP = N_HEADS // N_GROUPS
D_STATE = 128
SSM_CONV_WIDTH = 4
CONV_DIM = D_INNER + 2 * N_GROUPS * D_STATE
N_IN = 3 * C_A + D_INNER + CONV_DIM + N_HEADS + 2 * D_MODEL
NORM_EPS = 1e-6

kernel_name = "conformer_conv_ssd_gated_hybrid_step"


def rms_norm(x, w):
    xf = x.astype(jnp.float32)
    y = xf * lax.rsqrt(jnp.mean(xf * xf, axis=-1, keepdims=True) + NORM_EPS)
    return (y * w.astype(jnp.float32)).astype(x.dtype)


def layer_norm(x, w, b):
    xf = x.astype(jnp.float32)
    mu = jnp.mean(xf, axis=-1, keepdims=True)
    var = jnp.mean(jnp.square(xf - mu), axis=-1, keepdims=True)
    y = (xf - mu) * lax.rsqrt(var + NORM_EPS)
    return (y * w.astype(jnp.float32) + b.astype(jnp.float32)).astype(x.dtype)


def causal_dwconv(xp, w):
    return lax.conv_general_dilated(
        xp, w.astype(xp.dtype)[:, None, :], window_strides=(1,), padding='VALID',
        dimension_numbers=('NWC', 'WIO', 'NWC'), feature_group_count=xp.shape[-1])


def ssd_chunked(xs, dt, a_head, bm, cm, s0, q):
    b, t, h, p = xs.shape
    nc = t // q
    f32 = jnp.float32
    xd = (xs.astype(f32) * dt[..., None]).reshape(b, nc, q, N_GROUPS, HEADS_PER_GROUP, p)
    a = (dt * a_head).reshape(b, nc, q, N_GROUPS, HEADS_PER_GROUP).transpose(0, 1, 3, 4, 2)
    bm = bm.astype(f32).reshape(b, nc, q, N_GROUPS, D_STATE)
    cm = cm.astype(f32).reshape(b, nc, q, N_GROUPS, D_STATE)
    a_cum = jnp.cumsum(a, axis=-1)
    causal = jnp.tril(jnp.ones((q, q), dtype=bool))
    seg = a_cum[..., :, None] - a_cum[..., None, :]
    lmat = jnp.exp(jnp.where(causal, seg, -jnp.inf))
    cb = jnp.einsum('bclgn,bcsgn->bcgls', cm, bm)
    y_diag = jnp.einsum('bcgls,bcgrls,bcsgrp->bclgrp', cb, lmat, xd)
    decay_to_end = jnp.exp(a_cum[..., -1:] - a_cum)
    chunk_states = jnp.einsum('bclgn,bcgrl,bclgrp->bcgrpn', bm, decay_to_end, xd)
    chunk_decay = jnp.exp(a_cum[..., -1])

    def step(s, inp):
        cs, cd = inp
        return s * cd[..., None, None] + cs, s

    s_init = s0.astype(f32).reshape(b, N_GROUPS, HEADS_PER_GROUP, p, D_STATE)
    s_final, s_in = lax.scan(step, s_init,
                             (jnp.moveaxis(chunk_states, 1, 0), jnp.moveaxis(chunk_decay, 1, 0)))
    s_in = jnp.moveaxis(s_in, 0, 1)
    y_off = jnp.einsum('bclgn,bcgrpn,bcgrl->bclgrp', cm, s_in, jnp.exp(a_cum))
    y = (y_diag + y_off).reshape(b, t, h, p)
    return y, s_final.reshape(b, h, p, D_STATE)


def hybrid_layer(x, conv_a_hist, conv_s_hist, ssm0, q,
                 norm_w, w_in, b_gate, conv_a_w, conv_a_b, ln_a_w, ln_a_b, w_a_out,
                 conv_s_w, conv_s_b, dt_bias, a_log, d_skip, gnorm_w, w_b_out, w_out):
    b, t, _ = x.shape
    h = rms_norm(x, norm_w)
    proj = jnp.einsum('btd,dn->btn', h, w_in.astype(h.dtype))
    o = 0
    a_val = proj[..., o:o + C_A]; o += C_A
    a_glu = proj[..., o:o + C_A]; o += C_A
    a_z = proj[..., o:o + C_A]; o += C_A
    s_z = proj[..., o:o + D_INNER]; o += D_INNER
    xbc = proj[..., o:o + CONV_DIM]; o += CONV_DIM
    dt_raw = proj[..., o:o + N_HEADS]; o += N_HEADS
    gates = proj[..., o:o + 2 * D_MODEL]

    u = a_val * jax.nn.sigmoid(a_glu)
    ua = jnp.concatenate([conv_a_hist.astype(u.dtype), u], axis=1)
    new_conv_a = ua[:, -(CONV_A_WIDTH - 1):]
    c = causal_dwconv(ua, conv_a_w) + conv_a_b.astype(u.dtype)
    c = layer_norm(c, ln_a_w, ln_a_b)
    c = jax.nn.silu(c) * jax.nn.silu(a_z)
    a_out = jnp.einsum('btc,cd->btd', c, w_a_out.astype(c.dtype))

    xp = jnp.concatenate([conv_s_hist.astype(xbc.dtype), xbc], axis=1)
    new_conv_s = xp[:, -(SSM_CONV_WIDTH - 1):]
    xbc = jax.nn.silu(causal_dwconv(xp, conv_s_w) + conv_s_b.astype(xbc.dtype))
    xs = xbc[..., :D_INNER].reshape(b, t, N_HEADS, HEAD_DIM)
    bm = xbc[..., D_INNER:D_INNER + N_GROUPS * D_STATE].reshape(b, t, N_GROUPS, D_STATE)
    cm = xbc[..., D_INNER + N_GROUPS * D_STATE:].reshape(b, t, N_GROUPS, D_STATE)
    dt = jax.nn.softplus(dt_raw.astype(jnp.float32) + dt_bias.astype(jnp.float32))
    a_head = -jnp.exp(a_log.astype(jnp.float32))
    y, s_new = ssd_chunked(xs, dt, a_head, bm, cm, ssm0, q)
    y = y + d_skip.astype(jnp.float32)[:, None] * xs.astype(jnp.float32)
    yg = (y.reshape(b, t, D_INNER) * jax.nn.silu(s_z.astype(jnp.float32)))
    yg = yg.reshape(b, t, N_GROUPS, D_INNER // N_GROUPS)
    yg = yg * lax.rsqrt(jnp.mean(yg * yg, axis=-1, keepdims=True) + NORM_EPS)
    yg = (yg.reshape(b, t, D_INNER) * gnorm_w.astype(jnp.float32)).astype(x.dtype)
    b_out = jnp.einsum('bti,id->btd', yg, w_b_out.astype(yg.dtype))

    g = jax.nn.sigmoid(gates + b_gate.astype(gates.dtype))
    m = g[..., :D_MODEL] * a_out + g[..., D_MODEL:] * b_out
    x = x + jnp.einsum('btd,de->bte', m, w_out.astype(m.dtype))
    return x, new_conv_a, new_conv_s, s_new.astype(ssm0.dtype)


def run_trunk(x, conv_a_states, conv_s_states, ssm_states, q, norm_w, w_in, b_gate, conv_a_w,
              conv_a_b, ln_a_w, ln_a_b, w_a_out, conv_s_w, conv_s_b, dt_bias, a_log, d_skip,
              gnorm_w, w_b_out, w_out, final_norm_w):
    new_a, new_s, new_ssm = [], [], []
    for l in range(DEPTH):
        x, ca, cs, ss = hybrid_layer(
            x, conv_a_states[l], conv_s_states[l], ssm_states[l], q,
            norm_w[l], w_in[l], b_gate[l], conv_a_w[l], conv_a_b[l], ln_a_w[l], ln_a_b[l],
            w_a_out[l], conv_s_w[l], conv_s_b[l], dt_bias[l], a_log[l], d_skip[l], gnorm_w[l],
            w_b_out[l], w_out[l])
        new_a.append(ca); new_s.append(cs); new_ssm.append(ss)
    return rms_norm(x, final_norm_w), jnp.stack(new_a), jnp.stack(new_s), jnp.stack(new_ssm)


def setup_inputs(seed: int = 0) -> dict:
    key = jax.random.key(seed)
    ks = jax.random.split(key, 24)
    f32 = jnp.float32
    nrm = lambda k, s, sc: jax.random.normal(k, s, f32) * sc
    dt0 = jnp.exp(jax.random.uniform(ks[15], (DEPTH, N_HEADS), f32, math.log(1e-3), math.log(1e-1)))
    return {
        "x_prompt": nrm(ks[0], (BATCH, SEQ, D_MODEL), 1.0),
        "x_sample": nrm(ks[1], (DEC_BATCH, DEC_SEQ, D_MODEL), 1.0),
        "state_conv_a": nrm(ks[2], (DEPTH, DEC_BATCH, CONV_A_WIDTH - 1, C_A), 1.0),
        "state_conv_s": nrm(ks[3], (DEPTH, DEC_BATCH, SSM_CONV_WIDTH - 1, CONV_DIM), 1.0),
        "state_ssm": nrm(ks[4], (DEPTH, DEC_BATCH, N_HEADS, HEAD_DIM, D_STATE), 0.5),
        "norm_w": 1.0 + nrm(ks[5], (DEPTH, D_MODEL), 0.02),
        "w_in": nrm(ks[6], (DEPTH, D_MODEL, N_IN), D_MODEL ** -0.5),
        "b_gate": nrm(ks[7], (DEPTH, 2 * D_MODEL), 0.02),
        "conv_a_w": nrm(ks[8], (DEPTH, CONV_A_WIDTH, C_A), CONV_A_WIDTH ** -0.5),
        "conv_a_b": nrm(ks[9], (DEPTH, C_A), 0.02),
        "ln_a_w": 1.0 + nrm(ks[10], (DEPTH, C_A), 0.02),
        "ln_a_b": nrm(ks[11], (DEPTH, C_A), 0.02),
        "w_a_out": nrm(ks[12], (DEPTH, C_A, D_MODEL), C_A ** -0.5),
        "conv_s_w": nrm(ks[13], (DEPTH, SSM_CONV_WIDTH, CONV_DIM), SSM_CONV_WIDTH ** -0.5),
        "conv_s_b": nrm(ks[14], (DEPTH, CONV_DIM), 0.02),
        "dt_bias": dt0 + jnp.log(-jnp.expm1(-dt0)),
        "a_log": jnp.log(jax.random.uniform(ks[16], (DEPTH, N_HEADS), f32, 1.0, 16.0)),
        "d_skip": 1.0 + nrm(ks[17], (DEPTH, N_HEADS), 0.02),
        "gnorm_w": 1.0 + nrm(ks[18], (DEPTH, D_INNER), 0.02),
        "w_b_out": nrm(ks[19], (DEPTH, D_INNER, D_MODEL), D_INNER ** -0.5),
        "w_out": nrm(ks[20], (DEPTH, D_MODEL, D_MODEL), D_MODEL ** -0.5),
        "final_norm_w": 1.0 + nrm(ks[21], (D_MODEL,), 0.02),
    }


def reference(x_prompt, x_sample, state_conv_a, state_conv_s, state_ssm, norm_w, w_in, b_gate,
              conv_a_w, conv_a_b, ln_a_w, ln_a_b, w_a_out, conv_s_w, conv_s_b, dt_bias, a_log,
              d_skip, gnorm_w, w_b_out, w_out, final_norm_w):
    dt_ = x_prompt.dtype
    zero_a = jnp.zeros((DEPTH, BATCH, CONV_A_WIDTH - 1, C_A), dt_)
    zero_s = jnp.zeros((DEPTH, BATCH, SSM_CONV_WIDTH - 1, CONV_DIM), dt_)
    zero_ssm = jnp.zeros((DEPTH, BATCH, N_HEADS, HEAD_DIM, D_STATE), dt_)
    y_prompt, conv_a_prompt, conv_s_prompt, ssm_prompt = run_trunk(
        x_prompt, zero_a, zero_s, zero_ssm, CHUNK, norm_w, w_in, b_gate, conv_a_w, conv_a_b,
        ln_a_w, ln_a_b, w_a_out, conv_s_w, conv_s_b, dt_bias, a_log, d_skip, gnorm_w, w_b_out,
        w_out, final_norm_w)
    y_sample, conv_a_sample, conv_s_sample, ssm_sample = run_trunk(
        x_sample, state_conv_a, state_conv_s, state_ssm, x_sample.shape[1], norm_w, w_in, b_gate,
        conv_a_w, conv_a_b, ln_a_w, ln_a_b, w_a_out, conv_s_w, conv_s_b, dt_bias, a_log, d_skip,
        gnorm_w, w_b_out, w_out, final_norm_w)
    return (y_prompt, y_sample, conv_a_prompt, conv_s_prompt, ssm_prompt,
            conv_a_sample, conv_s_sample, ssm_sample)
```

```python
import functools

import jax
import jax.numpy as jnp
from jax import lax
from jax.experimental import pallas as pl
from jax.experimental.pallas import tpu as pltpu

F32 = jnp.float32
BF16 = jnp.bfloat16

D_MODEL = 2048
C_A = 2048
D_INNER = 4096
N_HEADS = 64
HEAD_DIM = 64
N_GROUPS = 8
D_STATE = 128
CONV_DIM = D_INNER + 2 * N_GROUPS * D_STATE
CONV_A_WIDTH = 31
SSM_CONV_WIDTH = 4
CHUNK = 64
NORM_EPS = 1e-6

N_MAIN = 20480
COL_SZ = 0
COL_XS = 4096
COL_BC = 8192
COL_AVAL = 10240
COL_AGLU = 12288
COL_AZ = 14336
COL_GATE = 16384

A_HIST_ROWS = 32
S_HIST_ROWS = 8
PACK = 256
GROUP_LANES = D_INNER // N_GROUPS
VMEM_LIMIT = 48 * 1024 * 1024


def _cparams(sem):
    return pltpu.CompilerParams(dimension_semantics=sem, vmem_limit_bytes=VMEM_LIMIT)


def _rmsnorm_kernel(x_ref, w_ref, o_ref):
    x = x_ref[...]
    ms = jnp.mean(x * x, axis=-1, keepdims=True)
    o_ref[...] = (x * lax.rsqrt(ms + NORM_EPS) * w_ref[...]).astype(o_ref.dtype)


def _rmsnorm(x, w, out_dtype, tm=512):
    n, d = x.shape
    return pl.pallas_call(
        _rmsnorm_kernel,
        grid=(n // tm,),
        in_specs=[pl.BlockSpec((tm, d), lambda i: (i, 0)),
                  pl.BlockSpec((1, d), lambda i: (0, 0))],
        out_specs=pl.BlockSpec((tm, d), lambda i: (i, 0)),
        out_shape=jax.ShapeDtypeStruct((n, d), out_dtype),
        compiler_params=_cparams(("parallel",)),
        name="rmsnorm",
    )(x, w.reshape(1, d))


def _mm_kernel(a_ref, b_ref, o_ref):
    o_ref[...] = jnp.dot(a_ref[...], b_ref[...], preferred_element_type=F32).astype(o_ref.dtype)


def _in_proj(h, w, tm=1024, tn=1024):
    n, k = h.shape
    _, m = w.shape
    return pl.pallas_call(
        _mm_kernel,
        grid=(n // tm, m // tn),
        in_specs=[pl.BlockSpec((tm, k), lambda i, j: (i, 0)),
                  pl.BlockSpec((k, tn), lambda i, j: (0, j))],
        out_specs=pl.BlockSpec((tm, tn), lambda i, j: (i, j)),
        out_shape=jax.ShapeDtypeStruct((n, m), F32),
        compiler_params=_cparams(("parallel", "parallel")),
        name="in_proj",
    )(h, w)


def _conv_a_kernel(aval_ref, aglu_ref, az_ref, hist_ref, cw_ref, cb_ref, lnw_ref, lnb_ref,
                   c_ref, st_ref, ubuf, cbuf, *, carry):
    i = pl.program_id(0)
    t = CHUNK
    hr = A_HIST_ROWS
    if carry:
        @pl.when(i == 0)
        def _():
            ubuf[0:hr, :] = hist_ref[0]

        @pl.when(i > 0)
        def _():
            ubuf[0:hr, :] = ubuf[t:t + hr, :]
    else:
        ubuf[0:hr, :] = hist_ref[0]

    ubuf[hr:hr + t, :] = aval_ref[...] * jax.nn.sigmoid(aglu_ref[...])

    base = hr - (CONV_A_WIDTH - 1)
    strip = 256
    for s in range(C_A // strip):
        sl = slice(s * strip, (s + 1) * strip)
        acc = jnp.zeros((t, strip), F32)
        for k in range(CONV_A_WIDTH):
            acc = acc + ubuf[base + k:base + k + t, sl] * cw_ref[k:k + 1, sl]
        cbuf[:, sl] = acc + cb_ref[:, sl]

    c = cbuf[...]
    mu = jnp.mean(c, axis=-1, keepdims=True)
    cc = c - mu
    var = jnp.mean(cc * cc, axis=-1, keepdims=True)
    y = cc * lax.rsqrt(var + NORM_EPS) * lnw_ref[...] + lnb_ref[...]
    z = az_ref[...]
    out = (y * jax.nn.sigmoid(y)) * (z * jax.nn.sigmoid(z))
    c_ref[...] = out.astype(c_ref.dtype)

    st_ref[0] = ubuf[hr + t - (CONV_A_WIDTH - 1):hr + t, :]


def _conv_a(proj, row0, n_seq, chunks_per_seq, hist, cw, cb, lnw, lnb):
    t = CHUNK
    carry = chunks_per_seq > 1
    assert carry or n_seq >= 1
    assert not (carry and n_seq != 1)
    n_chunks = n_seq * chunks_per_seq
    r0 = row0 // t
    w2 = C_A

    def col(cidx):
        return lambda i: (r0 + i, cidx)

    seq_of = (lambda i: (0, 0, 0)) if carry else (lambda i: (i, 0, 0))
    kern = functools.partial(_conv_a_kernel, carry=carry)
    return pl.pallas_call(
        kern,
        grid=(n_chunks,),
        in_specs=[pl.BlockSpec((t, w2), col(COL_AVAL // w2)),
                  pl.BlockSpec((t, w2), col(COL_AGLU // w2)),
                  pl.BlockSpec((t, w2), col(COL_AZ // w2)),
                  pl.BlockSpec((1, A_HIST_ROWS, C_A), seq_of),
                  pl.BlockSpec((32, C_A), lambda i: (0, 0)),
                  pl.BlockSpec((1, C_A), lambda i: (0, 0)),
                  pl.BlockSpec((1, C_A), lambda i: (0, 0)),
                  pl.BlockSpec((1, C_A), lambda i: (0, 0))],
        out_specs=[pl.BlockSpec((t, C_A), lambda i: (i, 0)),
                   pl.BlockSpec((1, CONV_A_WIDTH - 1, C_A), seq_of)],
        out_shape=[jax.ShapeDtypeStruct((n_chunks * t, C_A), BF16),
                   jax.ShapeDtypeStruct((n_seq, CONV_A_WIDTH - 1, C_A), F32)],
        scratch_shapes=[pltpu.VMEM((A_HIST_ROWS + t, C_A), F32),
                        pltpu.VMEM((t, C_A), F32)],
        compiler_params=_cparams(("arbitrary",)),
        name="conv_a",
    )(proj, proj, proj, hist, cw, cb, lnw, lnb)


def _split_bf16(x, parts):
    out = []
    r = x
    for p in range(parts):
        hi = r.astype(BF16)
        out.append(hi)
        if p + 1 < parts:
            r = r - hi.astype(F32)
    return out


def _softplus(x):
    return jnp.maximum(x, 0.0) + jnp.log1p(jnp.exp(-jnp.abs(x)))


def _ssd_kernel(sz_ref, xs_ref, bc_ref, h_ref, hist_ref, st0_ref, wdt_ref, cw_ref, cb_ref,
                dtb_ref, alog_ref, dskip_ref, gw_ref,
                yg_ref, cst_ref, stout_ref,
                xbuf, xs_s, bc_s, st, e_s, acol_s, dtx_s, ygbuf, *, carry):
    i = pl.program_id(0)
    last = pl.num_programs(0) - 1
    t = CHUNK
    hr = S_HIST_ROWS
    nbc = 2 * N_GROUPS * D_STATE

    @pl.when(i == 0)
    def _():
        hh = lax.broadcasted_iota(jnp.int32, (N_HEADS, D_INNER), 0)
        jj = lax.broadcasted_iota(jnp.int32, (N_HEADS, D_INNER), 1)
        e_s[...] = jnp.where((jj >> 6) == hh, 1.0, 0.0).astype(BF16)

    if carry:
        @pl.when(i == 0)
        def _():
            xbuf[0:hr, :] = hist_ref[0]
            st[...] = st0_ref[0]

        @pl.when(i > 0)
        def _():
            xbuf[0:hr, :] = xbuf[t:t + hr, :]
    else:
        xbuf[0:hr, :] = hist_ref[0]
        st[...] = st0_ref[0]

    xbuf[hr:hr + t, 0:D_INNER] = xs_ref[...]
    xbuf[hr:hr + t, D_INNER:CONV_DIM] = bc_ref[...]

    base = hr - (SSM_CONV_WIDTH - 1)
    strip = 512
    for s in range(CONV_DIM // strip):
        sl = slice(s * strip, (s + 1) * strip)
        acc = jnp.zeros((t, strip), F32)
        for k in range(SSM_CONV_WIDTH):
            acc = acc + xbuf[base + k:base + k + t, sl] * cw_ref[k:k + 1, sl]
        acc = acc + cb_ref[:, sl]
        acc = acc * jax.nn.sigmoid(acc)
        if s * strip < D_INNER:
            xs_s[:, sl] = acc
        else:
            bc_s[:, s * strip - D_INNER:(s + 1) * strip - D_INNER] = acc

    dt_raw = jnp.dot(h_ref[...], wdt_ref[...], preferred_element_type=F32)
    dt = _softplus(dt_raw + dtb_ref[...])
    a = dt * (-jnp.exp(alog_ref[...]))
    ri = lax.broadcasted_iota(jnp.int32, (t, t), 0)
    ci = lax.broadcasted_iota(jnp.int32, (t, t), 1)
    tri = jnp.where(ci <= ri, 1.0, 0.0).astype(BF16)
    acum = jnp.zeros((t, N_HEADS), F32)
    for part in _split_bf16(a, 3):
        acum = acum + jnp.dot(tri, part, preferred_element_type=F32)

    stack = jnp.concatenate(_split_bf16(acum, 3) + _split_bf16(dt, 2), axis=0)
    r = jnp.dot(stack, e_s[...], preferred_element_type=F32)
    acol_s[...] = r[0:t] + r[t:2 * t] + r[2 * t:3 * t]
    dtx_s[...] = r[3 * t:4 * t] + r[4 * t:5 * t]

    li = lax.broadcasted_iota(jnp.int32, (t, PACK), 0)
    ji = lax.broadcasted_iota(jnp.int32, (t, PACK), 1)
    si = ji & (HEAD_DIM - 1)
    hi4 = ji >> 6
    diag_mask = li == si
    causal_mask = li >= si

    def group_body(g, carry_):
        goff = pl.multiple_of(g * GROUP_LANES, GROUP_LANES)
        boff = pl.multiple_of(g * D_STATE, D_STATE)
        bm = bc_s[:, pl.ds(boff, D_STATE)].astype(BF16)
        cm = bc_s[:, pl.ds(N_GROUPS * D_STATE + boff, D_STATE)].astype(BF16)
        bm4 = jnp.concatenate([bm, bm, bm, bm], axis=0)
        cb_rep = lax.dot_general(cm, bm4, (((1,), (1,)), ((), ())),
                                 preferred_element_type=F32)
        st_g = st[:, pl.ds(goff, GROUP_LANES)]
        yoff = jnp.dot(cm, st_g.astype(BF16), preferred_element_type=F32)
        ss = jnp.zeros((t, 1), F32)
        for half in range(GROUP_LANES // PACK):
            off = pl.multiple_of(goff + half * PACK, PACK)
            lanes = pl.ds(off, PACK)
            acol = acol_s[:, lanes]
            xs = xs_s[:, lanes]
            xd = xs * dtx_s[:, lanes]
            arow = jnp.sum(jnp.where(diag_mask, acol, 0.0), axis=0, keepdims=True)
            lmat = jnp.exp(jnp.where(causal_mask, acol - arow, -1e30))
            gmat = (cb_rep * lmat).astype(BF16)
            bd = jnp.concatenate(
                [jnp.where(hi4 == rr, xd, 0.0).astype(BF16) for rr in range(PACK // HEAD_DIM)],
                axis=0)
            ydiag = jnp.dot(gmat, bd, preferred_element_type=F32)
            y = (ydiag + jnp.exp(acol) * yoff[:, half * PACK:(half + 1) * PACK]
                 + dskip_ref[:, lanes] * xs)
            z = sz_ref[:, lanes]
            yg = y * (z * jax.nn.sigmoid(z))
            ygbuf[:, lanes] = yg
            ss = ss + jnp.sum(yg * yg, axis=-1, keepdims=True)
            alast = acol[t - 1:t, :]
            xdd = (xd * jnp.exp(alast - acol)).astype(BF16)
            upd = lax.dot_general(bm, xdd, (((0,), (0,)), ((), ())),
                                  preferred_element_type=F32)
            st[:, lanes] = st[:, lanes] * jnp.exp(alast) + upd
        scale = lax.rsqrt(ss * (1.0 / GROUP_LANES) + NORM_EPS)
        glanes = pl.ds(goff, GROUP_LANES)
        yg_ref[:, glanes] = (ygbuf[:, glanes] * scale * gw_ref[:, glanes]).astype(yg_ref.dtype)
        return carry_

    lax.fori_loop(0, N_GROUPS, group_body, 0)

    def write_state():
        cst_ref[0] = xbuf[hr + t - (SSM_CONV_WIDTH - 1):hr + t, :]
        stout_ref[0] = st[...]

    if carry:
        pl.when(i == last)(write_state)
    else:
        write_state()


def _ssd(proj, h, row0, n_seq, chunks_per_seq, hist, st0, wdt, cw, cb, dtb, alog, dskip, gw):
    t = CHUNK
    carry = chunks_per_seq > 1
    assert not (carry and n_seq != 1)
    n_chunks = n_seq * chunks_per_seq
    r0 = row0 // t
    seq_of = (lambda i: (0, 0, 0)) if carry else (lambda i: (i, 0, 0))
    const2 = lambda i: (0, 0)
    kern = functools.partial(_ssd_kernel, carry=carry)
    return pl.pallas_call(
        kern,
        grid=(n_chunks,),
        in_specs=[pl.BlockSpec((t, D_INNER), lambda i: (r0 + i, COL_SZ // D_INNER)),
                  pl.BlockSpec((t, D_INNER), lambda i: (r0 + i, COL_XS // D_INNER)),
                  pl.BlockSpec((t, 2048), lambda i: (r0 + i, COL_BC // 2048)),
                  pl.BlockSpec((t, D_MODEL), lambda i: (r0 + i, 0)),
                  pl.BlockSpec((1, S_HIST_ROWS, CONV_DIM), seq_of),
                  pl.BlockSpec((1, D_STATE, D_INNER), seq_of),
                  pl.BlockSpec((D_MODEL, N_HEADS), const2),
                  pl.BlockSpec((8, CONV_DIM), const2),
                  pl.BlockSpec((1, CONV_DIM), const2),
                  pl.BlockSpec((1, N_HEADS), const2),
                  pl.BlockSpec((1, N_HEADS), const2),
                  pl.BlockSpec((1, D_INNER), const2),
                  pl.BlockSpec((1, D_INNER), const2)],
        out_specs=[pl.BlockSpec((t, D_INNER), lambda i: (i, 0)),
                   pl.BlockSpec((1, SSM_CONV_WIDTH - 1, CONV_DIM), seq_of),
                   pl.BlockSpec((1, D_STATE, D_INNER), seq_of)],
        out_shape=[jax.ShapeDtypeStruct((n_chunks * t, D_INNER), BF16),
                   jax.ShapeDtypeStruct((n_seq, SSM_CONV_WIDTH - 1, CONV_DIM), F32),
                   jax.ShapeDtypeStruct((n_seq, D_STATE, D_INNER), F32)],
        scratch_shapes=[pltpu.VMEM((S_HIST_ROWS + t, CONV_DIM), F32),
                        pltpu.VMEM((t, D_INNER), F32),
                        pltpu.VMEM((t, 2 * N_GROUPS * D_STATE), F32),
                        pltpu.VMEM((D_STATE, D_INNER), F32),
                        pltpu.VMEM((N_HEADS, D_INNER), BF16),
                        pltpu.VMEM((t, D_INNER), F32),
                        pltpu.VMEM((t, D_INNER), F32),
                        pltpu.VMEM((t, D_INNER), F32)],
        compiler_params=_cparams(("arbitrary",)),
        name="ssd",
    )(proj, proj, proj, h, hist, st0, wdt, cw, cb, dtb, alog, dskip, gw)


def _merge_kernel(c_ref, yg_ref, ga_ref, gb_ref, bga_ref, bgb_ref, wa_ref, wb_ref, m_ref):
    a_out = jnp.dot(c_ref[...], wa_ref[...], preferred_element_type=F32)
    b_out = jnp.dot(yg_ref[...], wb_ref[...], preferred_element_type=F32)
    ga = jax.nn.sigmoid(ga_ref[...] + bga_ref[...])
    gb = jax.nn.sigmoid(gb_ref[...] + bgb_ref[...])
    m_ref[...] = (ga * a_out + gb * b_out).astype(m_ref.dtype)


def _merge(c, yg, proj, bg, wa, wb, tm=256, tn=1024):
    n = c.shape[0]
    gcol = COL_GATE // tn
    nj = D_MODEL // tn
    return pl.pallas_call(
        _merge_kernel,
        grid=(nj, n // tm),
        in_specs=[pl.BlockSpec((tm, C_A), lambda j, i: (i, 0)),
                  pl.BlockSpec((tm, D_INNER), lambda j, i: (i, 0)),
                  pl.BlockSpec((tm, tn), lambda j, i: (i, gcol + j)),
                  pl.BlockSpec((tm, tn), lambda j, i: (i, gcol + nj + j)),
                  pl.BlockSpec((1, tn), lambda j, i: (0, j)),
                  pl.BlockSpec((1, tn), lambda j, i: (0, nj + j)),
                  pl.BlockSpec((C_A, tn), lambda j, i: (0, j)),
                  pl.BlockSpec((D_INNER, tn), lambda j, i: (0, j))],
        out_specs=pl.BlockSpec((tm, tn), lambda j, i: (i, j)),
        out_shape=jax.ShapeDtypeStruct((n, D_MODEL), BF16),
        compiler_params=_cparams(("parallel", "parallel")),
        name="merge",
    )(c, yg, proj, proj, bg, bg, wa, wb)


def _out_kernel(m_ref, w_ref, x_ref, o_ref):
    o_ref[...] = x_ref[...] + jnp.dot(m_ref[...], w_ref[...], preferred_element_type=F32)


def _out_proj(m, w, x, tm=512):
    n = m.shape[0]
    return pl.pallas_call(
        _out_kernel,
        grid=(n // tm,),
        in_specs=[pl.BlockSpec((tm, D_MODEL), lambda i: (i, 0)),
                  pl.BlockSpec((D_MODEL, D_MODEL), lambda i: (0, 0)),
                  pl.BlockSpec((tm, D_MODEL), lambda i: (i, 0))],
        out_specs=pl.BlockSpec((tm, D_MODEL), lambda i: (i, 0)),
        out_shape=jax.ShapeDtypeStruct((n, D_MODEL), F32),
        compiler_params=_cparams(("parallel",)),
        name="out_proj",
    )(m, w, x)


def _pad_rows_front(x, rows):
    pad = rows - x.shape[-2]
    cfg = [(0, 0)] * (x.ndim - 2) + [(pad, 0), (0, 0)]
    return jnp.pad(x, cfg)


def _pad_rows_back(x, rows):
    pad = rows - x.shape[-2]
    cfg = [(0, 0)] * (x.ndim - 2) + [(0, pad), (0, 0)]
    return jnp.pad(x, cfg)


def kernel(x_prompt, x_sample, state_conv_a, state_conv_s, state_ssm, norm_w, w_in, b_gate,
           conv_a_w, conv_a_b, ln_a_w, ln_a_b, w_a_out, conv_s_w, conv_s_b, dt_bias, a_log,
           d_skip, gnorm_w, w_b_out, w_out, final_norm_w):
    depth = w_in.shape[0]
    batch, seq, _ = x_prompt.shape
    dec_batch, dec_seq, _ = x_sample.shape
    assert batch == 1 and dec_seq == CHUNK and seq % CHUNK == 0
    n_p = batch * seq
    n_s = dec_batch * dec_seq

    o_az = 2 * C_A
    o_sz = 3 * C_A
    o_xbc = o_sz + D_INNER
    o_dt = o_xbc + CONV_DIM
    o_gate = o_dt + N_HEADS

    x = jnp.concatenate([x_prompt.reshape(n_p, D_MODEL), x_sample.reshape(n_s, D_MODEL)], axis=0)

    zero_hist_a = jnp.zeros((1, A_HIST_ROWS, C_A), F32)
    zero_hist_s = jnp.zeros((1, S_HIST_ROWS, CONV_DIM), F32)
    zero_st = jnp.zeros((1, D_STATE, D_INNER), F32)

    conv_a_p, conv_s_p, ssm_p, conv_a_s, conv_s_s, ssm_s = [], [], [], [], [], []
    for l in range(depth):
        wl = w_in[l]
        w_main = jnp.concatenate(
            [wl[:, o_sz:o_dt], wl[:, 0:o_sz], wl[:, o_gate:]], axis=1).astype(BF16)
        w_dt = wl[:, o_dt:o_gate].astype(BF16)
        cw_a = _pad_rows_back(conv_a_w[l], 32)
        cw_s = _pad_rows_back(conv_s_w[l], 8)
        cb_a = conv_a_b[l].reshape(1, C_A)
        lnw = ln_a_w[l].reshape(1, C_A)
        lnb = ln_a_b[l].reshape(1, C_A)
        cb_s = conv_s_b[l].reshape(1, CONV_DIM)
        dtb = dt_bias[l].reshape(1, N_HEADS)
        alog = a_log[l].reshape(1, N_HEADS)
        dskip = jnp.repeat(d_skip[l], HEAD_DIM).reshape(1, D_INNER)
        gw = gnorm_w[l].reshape(1, D_INNER)
        bg = b_gate[l].reshape(1, 2 * D_MODEL)
        hist_a = _pad_rows_front(state_conv_a[l], A_HIST_ROWS)
        hist_s = _pad_rows_front(state_conv_s[l], S_HIST_ROWS)
        st0 = state_ssm[l].reshape(dec_batch, D_INNER, D_STATE).transpose(0, 2, 1)

        h = _rmsnorm(x, norm_w[l], BF16)
        proj = _in_proj(h, w_main)

        c_p, ca_p = _conv_a(proj, 0, 1, seq // CHUNK, zero_hist_a, cw_a, cb_a, lnw, lnb)
        c_s, ca_s = _conv_a(proj, n_p, dec_batch, 1, hist_a, cw_a, cb_a, lnw, lnb)
        yg_p, cs_p, st_p = _ssd(proj, h, 0, 1, seq // CHUNK, zero_hist_s, zero_st,
                                w_dt, cw_s, cb_s, dtb, alog, dskip, gw)
        yg_s, cs_s, st_s = _ssd(proj, h, n_p, dec_batch, 1, hist_s, st0,
                                w_dt, cw_s, cb_s, dtb, alog, dskip, gw)
        c = jnp.concatenate([c_p, c_s], axis=0)
        yg = jnp.concatenate([yg_p, yg_s], axis=0)

        m = _merge(c, yg, proj, bg, w_a_out[l].astype(BF16), w_b_out[l].astype(BF16))
        x = _out_proj(m, w_out[l].astype(BF16), x)

        conv_a_p.append(ca_p)
        conv_a_s.append(ca_s)
        conv_s_p.append(cs_p)
        conv_s_s.append(cs_s)
        ssm_p.append(st_p.transpose(0, 2, 1).reshape(batch, N_HEADS, HEAD_DIM, D_STATE))
        ssm_s.append(st_s.transpose(0, 2, 1).reshape(dec_batch, N_HEADS, HEAD_DIM, D_STATE))

    y = _rmsnorm(x, final_norm_w, F32)
    y_prompt = y[:n_p].reshape(batch, seq, D_MODEL)
    y_sample = y[n_p:].reshape(dec_batch, dec_seq, D_MODEL)
    return (y_prompt, y_sample, jnp.stack(conv_a_p), jnp.stack(conv_s_p), jnp.stack(ssm_p),
            jnp.stack(conv_a_s), jnp.stack(conv_s_s), jnp.stack(ssm_s))
```

```python
import functools

import jax
import jax.numpy as jnp
from jax import lax
from jax.experimental import pallas as pl
from jax.experimental.pallas import tpu as pltpu

F32 = jnp.float32
BF16 = jnp.bfloat16

D_MODEL = 2048
C_A = 2048
D_INNER = 4096
N_HEADS = 64
HEAD_DIM = 64
N_GROUPS = 8
D_STATE = 128
CONV_DIM = D_INNER + 2 * N_GROUPS * D_STATE
CONV_A_WIDTH = 31
SSM_CONV_WIDTH = 4
CHUNK = 64
NORM_EPS = 1e-6

COL_SZ = 0
COL_XS = 4096
COL_BC = 8192
COL_AVAL = 10240
COL_AGLU = 12288
COL_AZ = 14336
W_COL_SZ = 3 * C_A
W_COL_DT = W_COL_SZ + D_INNER + CONV_DIM
W_COL_GATE = W_COL_DT + N_HEADS

A_HIST_ROWS = 32
S_HIST_ROWS = 8
PACK = 256
GROUP_LANES = D_INNER // N_GROUPS
VMEM_LIMIT = 48 * 1024 * 1024


def _cparams(sem):
    return pltpu.CompilerParams(dimension_semantics=sem, vmem_limit_bytes=VMEM_LIMIT)


def _rmsnorm_kernel(x_ref, w_ref, o_ref):
    x = x_ref[...]
    ms = jnp.mean(x * x, axis=-1, keepdims=True)
    o_ref[...] = (x * lax.rsqrt(ms + NORM_EPS) * w_ref[...]).astype(o_ref.dtype)


def _rmsnorm(x, w, out_dtype, tm=512):
    n, d = x.shape
    return pl.pallas_call(
        _rmsnorm_kernel,
        grid=(n // tm,),
        in_specs=[pl.BlockSpec((tm, d), lambda i: (i, 0)),
                  pl.BlockSpec((1, d), lambda i: (0, 0))],
        out_specs=pl.BlockSpec((tm, d), lambda i: (i, 0)),
        out_shape=jax.ShapeDtypeStruct((n, d), out_dtype),
        compiler_params=_cparams(("parallel",)),
        name="rmsnorm",
    )(x, w.reshape(1, d))


def _in_proj_kernel(h_ref, w_ref, o_ref, wbf):
    @pl.when(pl.program_id(1) == 0)
    def _():
        wbf[...] = w_ref[...].astype(BF16)

    o_ref[...] = jnp.dot(h_ref[...], wbf[...], preferred_element_type=F32)


def _in_proj(h, w, n_cols, col_shift, tm=1024, tn=1024):
    n, k = h.shape
    n_w = n_cols // tn
    shift = col_shift // tn
    return pl.pallas_call(
        _in_proj_kernel,
        grid=(n_w, n // tm),
        in_specs=[pl.BlockSpec((tm, k), lambda j, i: (i, 0)),
                  pl.BlockSpec((k, tn), lambda j, i: (0, (j + shift) % n_w))],
        out_specs=pl.BlockSpec((tm, tn), lambda j, i: (i, j)),
        out_shape=jax.ShapeDtypeStruct((n, n_cols), F32),
        scratch_shapes=[pltpu.VMEM((k, tn), BF16)],
        compiler_params=_cparams(("arbitrary", "arbitrary")),
        name="in_proj",
    )(h, w)


def _conv_a_kernel(aval_ref, aglu_ref, az_ref, hist_ref, cw_ref, cb_ref, lnw_ref, lnb_ref,
                   c_ref, stp_ref, sts_ref, ubuf, cbuf, *, n_pc):
    i = pl.program_id(0)
    t = CHUNK
    hr = A_HIST_ROWS
    nh = CONV_A_WIDTH - 1

    @pl.when(i == 0)
    def _():
        ubuf[0:hr, :] = jnp.zeros((hr, C_A), F32)

    @pl.when((i > 0) & (i < n_pc))
    def _():
        ubuf[0:hr, :] = ubuf[t:t + hr, :]

    @pl.when(i >= n_pc)
    def _():
        ubuf[hr - nh:hr, :] = hist_ref[0]

    ubuf[hr:hr + t, :] = aval_ref[...] * jax.nn.sigmoid(aglu_ref[...])

    base = hr - nh
    strip = 256
    for s in range(C_A // strip):
        sl = slice(s * strip, (s + 1) * strip)
        acc = jnp.zeros((t, strip), F32)
        for k in range(CONV_A_WIDTH):
            acc = acc + ubuf[base + k:base + k + t, sl] * cw_ref[k:k + 1, sl]
        cbuf[:, sl] = acc + cb_ref[:, sl]

    c = cbuf[...]
    mu = jnp.mean(c, axis=-1, keepdims=True)
    cc = c - mu
    var = jnp.mean(cc * cc, axis=-1, keepdims=True)
    y = cc * lax.rsqrt(var + NORM_EPS) * lnw_ref[...] + lnb_ref[...]
    z = az_ref[...]
    out = (y * jax.nn.sigmoid(y)) * (z * jax.nn.sigmoid(z))
    c_ref[...] = out.astype(c_ref.dtype)

    @pl.when(i == n_pc - 1)
    def _():
        stp_ref[0] = ubuf[hr + t - nh:hr + t, :]

    @pl.when(i >= n_pc)
    def _():
        sts_ref[0] = ubuf[hr + t - nh:hr + t, :]


def _conv_a(proj, n_pc, n_sc, hist, cw, cb, lnw, lnb):
    t = CHUNK
    nh = CONV_A_WIDTH - 1
    n_chunks = n_pc + n_sc
    w2 = C_A
    seq_of = lambda i: (jnp.maximum(i - n_pc, 0), 0, 0)
    const2 = lambda i: (0, 0)
    kern = functools.partial(_conv_a_kernel, n_pc=n_pc)
    return pl.pallas_call(
        kern,
        grid=(n_chunks,),
        in_specs=[pl.BlockSpec((t, w2), lambda i: (i, COL_AVAL // w2)),
                  pl.BlockSpec((t, w2), lambda i: (i, COL_AGLU // w2)),
                  pl.BlockSpec((t, w2), lambda i: (i, COL_AZ // w2)),
                  pl.BlockSpec((1, nh, C_A), seq_of),
                  pl.BlockSpec((CONV_A_WIDTH, C_A), const2),
                  pl.BlockSpec((1, C_A), const2),
                  pl.BlockSpec((1, C_A), const2),
                  pl.BlockSpec((1, C_A), const2)],
        out_specs=[pl.BlockSpec((t, C_A), lambda i: (i, 0)),
                   pl.BlockSpec((1, nh, C_A), lambda i: (0, 0, 0)),
                   pl.BlockSpec((1, nh, C_A), seq_of)],
        out_shape=[jax.ShapeDtypeStruct((n_chunks * t, C_A), BF16),
                   jax.ShapeDtypeStruct((1, nh, C_A), F32),
                   jax.ShapeDtypeStruct((n_sc, nh, C_A), F32)],
        scratch_shapes=[pltpu.VMEM((A_HIST_ROWS + t, C_A), F32),
                        pltpu.VMEM((t, C_A), F32)],
        compiler_params=_cparams(("arbitrary",)),
        name="conv_a",
    )(proj, proj, proj, hist, cw, cb, lnw, lnb)


def _split_bf16(x, parts):
    out = []
    r = x
    for p in range(parts):
        hi = r.astype(BF16)
        out.append(hi)
        if p + 1 < parts:
            r = r - hi.astype(F32)
    return out


def _softplus(x):
    return jnp.maximum(x, 0.0) + jnp.log1p(jnp.exp(-jnp.abs(x)))


def _ssd_kernel(sz_ref, xs_ref, bc_ref, h_ref, hist_ref, st0_ref, wdt_ref, cw_ref, cb_ref,
                dtb_ref, alog_ref, dskip_ref, gw_ref,
                yg_ref, cstp_ref, csts_ref, stp_ref, sts_ref,
                xbuf, xs_s, bc_s, st, e_s, acol_s, dtx_s, ygbuf, *, n_pc):
    i = pl.program_id(0)
    t = CHUNK
    hr = S_HIST_ROWS
    nh = SSM_CONV_WIDTH - 1
    tb = 128

    @pl.when(i == 0)
    def _():
        hh = lax.broadcasted_iota(jnp.int32, (N_HEADS, D_INNER), 0)
        jj = lax.broadcasted_iota(jnp.int32, (N_HEADS, D_INNER), 1)
        e_s[...] = jnp.where((jj >> 6) == hh, 1.0, 0.0).astype(BF16)
        xbuf[0:hr, :] = jnp.zeros((hr, CONV_DIM), F32)
        st[...] = jnp.zeros((D_STATE, D_INNER), F32)

    @pl.when((i > 0) & (i < n_pc))
    def _():
        xbuf[0:hr, :] = xbuf[t:t + hr, :]

    @pl.when(i >= n_pc)
    def _():
        xbuf[hr - nh:hr, :] = hist_ref[0]
        for b in range(D_INNER // tb):
            st[:, b * tb:(b + 1) * tb] = st0_ref[0, b * tb:(b + 1) * tb, :].T

    xbuf[hr:hr + t, 0:D_INNER] = xs_ref[...]
    xbuf[hr:hr + t, D_INNER:CONV_DIM] = bc_ref[...]

    base = hr - nh
    strip = 512
    for s in range(CONV_DIM // strip):
        sl = slice(s * strip, (s + 1) * strip)
        acc = jnp.zeros((t, strip), F32)
        for k in range(SSM_CONV_WIDTH):
            acc = acc + xbuf[base + k:base + k + t, sl] * cw_ref[k:k + 1, sl]
        acc = acc + cb_ref[:, sl]
        acc = acc * jax.nn.sigmoid(acc)
        if s * strip < D_INNER:
            xs_s[:, sl] = acc
        else:
            bc_s[:, s * strip - D_INNER:(s + 1) * strip - D_INNER] = acc

    dt_raw = jnp.dot(h_ref[...], wdt_ref[...], preferred_element_type=F32)
    dt = _softplus(dt_raw + dtb_ref[...])
    a = dt * (-jnp.exp(alog_ref[...]))
    ri = lax.broadcasted_iota(jnp.int32, (t, t), 0)
    ci = lax.broadcasted_iota(jnp.int32, (t, t), 1)
    tri = jnp.where(ci <= ri, 1.0, 0.0).astype(BF16)
    acum = jnp.zeros((t, N_HEADS), F32)
    for part in _split_bf16(a, 3):
        acum = acum + jnp.dot(tri, part, preferred_element_type=F32)

    stack = jnp.concatenate(_split_bf16(acum, 3) + _split_bf16(dt, 2), axis=0)
    r = jnp.dot(stack, e_s[...], preferred_element_type=F32)
    acol_s[...] = r[0:t] + r[t:2 * t] + r[2 * t:3 * t]
    dtx_s[...] = r[3 * t:4 * t] + r[4 * t:5 * t]

    li = lax.broadcasted_iota(jnp.int32, (t, PACK), 0)
    ji = lax.broadcasted_iota(jnp.int32, (t, PACK), 1)
    si = ji & (HEAD_DIM - 1)
    hi4 = ji >> 6
    diag_mask = li == si
    causal_mask = li >= si

    def group_body(g, carry_):
        goff = pl.multiple_of(g * GROUP_LANES, GROUP_LANES)
        boff = pl.multiple_of(g * D_STATE, D_STATE)
        bm = bc_s[:, pl.ds(boff, D_STATE)].astype(BF16)
        cm = bc_s[:, pl.ds(N_GROUPS * D_STATE + boff, D_STATE)].astype(BF16)
        bm4 = jnp.concatenate([bm, bm, bm, bm], axis=0)
        cb_rep = lax.dot_general(cm, bm4, (((1,), (1,)), ((), ())),
                                 preferred_element_type=F32)
        st_g = st[:, pl.ds(goff, GROUP_LANES)]
        yoff = jnp.dot(cm, st_g.astype(BF16), preferred_element_type=F32)
        ss = jnp.zeros((t, 1), F32)
        for half in range(GROUP_LANES // PACK):
            off = pl.multiple_of(goff + half * PACK, PACK)
            lanes = pl.ds(off, PACK)
            acol = acol_s[:, lanes]
            xs = xs_s[:, lanes]
            xd = xs * dtx_s[:, lanes]
            arow = jnp.sum(jnp.where(diag_mask, acol, 0.0), axis=0, keepdims=True)
            lmat = jnp.exp(jnp.where(causal_mask, acol - arow, -1e30))
            gmat = (cb_rep * lmat).astype(BF16)
            bd = jnp.concatenate(
                [jnp.where(hi4 == rr, xd, 0.0).astype(BF16) for rr in range(PACK // HEAD_DIM)],
                axis=0)
            ydiag = jnp.dot(gmat, bd, preferred_element_type=F32)
            y = (ydiag + jnp.exp(acol) * yoff[:, half * PACK:(half + 1) * PACK]
                 + dskip_ref[:, lanes] * xs)
            z = sz_ref[:, lanes]
            yg = y * (z * jax.nn.sigmoid(z))
            ygbuf[:, lanes] = yg
            ss = ss + jnp.sum(yg * yg, axis=-1, keepdims=True)
            alast = acol[t - 1:t, :]
            xdd = (xd * jnp.exp(alast - acol)).astype(BF16)
            upd = lax.dot_general(bm, xdd, (((0,), (0,)), ((), ())),
                                  preferred_element_type=F32)
            st[:, lanes] = st[:, lanes] * jnp.exp(alast) + upd
        scale = lax.rsqrt(ss * (1.0 / GROUP_LANES) + NORM_EPS)
        glanes = pl.ds(goff, GROUP_LANES)
        yg_ref[:, glanes] = (ygbuf[:, glanes] * scale * gw_ref[:, glanes]).astype(yg_ref.dtype)
        return carry_

    lax.fori_loop(0, N_GROUPS, group_body, 0)

    def write_state(cst_ref, stout_ref):
        cst_ref[0] = xbuf[hr + t - nh:hr + t, :]
        for b in range(D_INNER // tb):
            stout_ref[0, b * tb:(b + 1) * tb, :] = st[:, b * tb:(b + 1) * tb].T

    @pl.when(i == n_pc - 1)
    def _():
        write_state(cstp_ref, stp_ref)

    @pl.when(i >= n_pc)
    def _():
        write_state(csts_ref, sts_ref)


def _ssd(proj, h, n_pc, n_sc, hist, st0, wdt, cw, cb, dtb, alog, dskip, gw):
    t = CHUNK
    nh = SSM_CONV_WIDTH - 1
    n_chunks = n_pc + n_sc
    seq_of = lambda i: (jnp.maximum(i - n_pc, 0), 0, 0)
    first = lambda i: (0, 0, 0)
    const2 = lambda i: (0, 0)
    kern = functools.partial(_ssd_kernel, n_pc=n_pc)
    return pl.pallas_call(
        kern,
        grid=(n_chunks,),
        in_specs=[pl.BlockSpec((t, D_INNER), lambda i: (i, COL_SZ // D_INNER)),
                  pl.BlockSpec((t, D_INNER), lambda i: (i, COL_XS // D_INNER)),
                  pl.BlockSpec((t, 2048), lambda i: (i, COL_BC // 2048)),
                  pl.BlockSpec((t, D_MODEL), lambda i: (i, 0)),
                  pl.BlockSpec((1, nh, CONV_DIM), seq_of),
                  pl.BlockSpec((1, D_INNER, D_STATE), seq_of),
                  pl.BlockSpec((D_MODEL, N_HEADS), const2),
                  pl.BlockSpec((SSM_CONV_WIDTH, CONV_DIM), const2),
                  pl.BlockSpec((1, CONV_DIM), const2),
                  pl.BlockSpec((1, N_HEADS), const2),
                  pl.BlockSpec((1, N_HEADS), const2),
                  pl.BlockSpec((1, D_INNER), const2),
                  pl.BlockSpec((1, D_INNER), const2)],
        out_specs=[pl.BlockSpec((t, D_INNER), lambda i: (i, 0)),
                   pl.BlockSpec((1, nh, CONV_DIM), first),
                   pl.BlockSpec((1, nh, CONV_DIM), seq_of),
                   pl.BlockSpec((1, D_INNER, D_STATE), first),
                   pl.BlockSpec((1, D_INNER, D_STATE), seq_of)],
        out_shape=[jax.ShapeDtypeStruct((n_chunks * t, D_INNER), BF16),
                   jax.ShapeDtypeStruct((1, nh, CONV_DIM), F32),
                   jax.ShapeDtypeStruct((n_sc, nh, CONV_DIM), F32),
                   jax.ShapeDtypeStruct((1, D_INNER, D_STATE), F32),
                   jax.ShapeDtypeStruct((n_sc, D_INNER, D_STATE), F32)],
        scratch_shapes=[pltpu.VMEM((S_HIST_ROWS + t, CONV_DIM), F32),
                        pltpu.VMEM((t, D_INNER), F32),
                        pltpu.VMEM((t, 2 * N_GROUPS * D_STATE), F32),
                        pltpu.VMEM((D_STATE, D_INNER), F32),
                        pltpu.VMEM((N_HEADS, D_INNER), BF16),
                        pltpu.VMEM((t, D_INNER), F32),
                        pltpu.VMEM((t, D_INNER), F32),
                        pltpu.VMEM((t, D_INNER), F32)],
        compiler_params=_cparams(("arbitrary",)),
        name="ssd",
    )(proj, proj, proj, h, hist, st0, wdt, cw, cb, dtb, alog, dskip, gw)


def _merge_kernel(c_ref, yg_ref, ga_ref, gb_ref, bga_ref, bgb_ref, wa_ref, wb_ref, m_ref,
                  wabf, wbbf):
    @pl.when(pl.program_id(1) == 0)
    def _():
        wabf[...] = wa_ref[...].astype(BF16)
        wbbf[...] = wb_ref[...].astype(BF16)

    a_out = jnp.dot(c_ref[...], wabf[...], preferred_element_type=F32)
    b_out = jnp.dot(yg_ref[...], wbbf[...], preferred_element_type=F32)
    ga = jax.nn.sigmoid(ga_ref[...] + bga_ref[...])
    gb = jax.nn.sigmoid(gb_ref[...] + bgb_ref[...])
    m_ref[...] = (ga * a_out + gb * b_out).astype(m_ref.dtype)


def _merge(c, yg, gates, bg, wa, wb, tm=256, tn=512):
    n = c.shape[0]
    nj = D_MODEL // tn
    return pl.pallas_call(
        _merge_kernel,
        grid=(nj, n // tm),
        in_specs=[pl.BlockSpec((tm, C_A), lambda j, i: (i, 0)),
                  pl.BlockSpec((tm, D_INNER), lambda j, i: (i, 0)),
                  pl.BlockSpec((tm, tn), lambda j, i: (i, j)),
                  pl.BlockSpec((tm, tn), lambda j, i: (i, nj + j)),
                  pl.BlockSpec((1, tn), lambda j, i: (0, j)),
                  pl.BlockSpec((1, tn), lambda j, i: (0, nj + j)),
                  pl.BlockSpec((C_A, tn), lambda j, i: (0, j)),
                  pl.BlockSpec((D_INNER, tn), lambda j, i: (0, j))],
        out_specs=pl.BlockSpec((tm, tn), lambda j, i: (i, j)),
        out_shape=jax.ShapeDtypeStruct((n, D_MODEL), BF16),
        scratch_shapes=[pltpu.VMEM((C_A, tn), BF16), pltpu.VMEM((D_INNER, tn), BF16)],
        compiler_params=_cparams(("arbitrary", "arbitrary")),
        name="merge",
    )(c, yg, gates, gates, bg, bg, wa, wb)


def _out_kernel(m_ref, w_ref, x_ref, o_ref, wbf):
    @pl.when(pl.program_id(1) == 0)
    def _():
        wbf[...] = w_ref[...].astype(BF16)

    o_ref[...] = x_ref[...] + jnp.dot(m_ref[...], wbf[...], preferred_element_type=F32)


def _out_proj(m, w, x, tm=512, tn=1024):
    n = m.shape[0]
    return pl.pallas_call(
        _out_kernel,
        grid=(D_MODEL // tn, n // tm),
        in_specs=[pl.BlockSpec((tm, D_MODEL), lambda j, i: (i, 0)),
                  pl.BlockSpec((D_MODEL, tn), lambda j, i: (0, j)),
                  pl.BlockSpec((tm, tn), lambda j, i: (i, j))],
        out_specs=pl.BlockSpec((tm, tn), lambda j, i: (i, j)),
        out_shape=jax.ShapeDtypeStruct((n, D_MODEL), F32),
        scratch_shapes=[pltpu.VMEM((D_MODEL, tn), BF16)],
        compiler_params=_cparams(("arbitrary", "arbitrary")),
        name="out_proj",
    )(m, w, x)


def kernel(x_prompt, x_sample, state_conv_a, state_conv_s, state_ssm, norm_w, w_in, b_gate,
           conv_a_w, conv_a_b, ln_a_w, ln_a_b, w_a_out, conv_s_w, conv_s_b, dt_bias, a_log,
           d_skip, gnorm_w, w_b_out, w_out, final_norm_w):
    depth = w_in.shape[0]
    batch, seq, _ = x_prompt.shape
    dec_batch, dec_seq, _ = x_sample.shape
    assert batch == 1 and dec_seq == CHUNK and seq % CHUNK == 0
    assert w_in.shape[2] == W_COL_GATE + 2 * D_MODEL
    n_p = batch * seq
    n_s = dec_batch * dec_seq
    n_pc = n_p // CHUNK

    x = jnp.concatenate([x_prompt.reshape(n_p, D_MODEL), x_sample.reshape(n_s, D_MODEL)], axis=0)
    w_gate = w_in[:, :, W_COL_GATE:]
    w_dt = w_in[:, :, W_COL_DT:W_COL_GATE].astype(BF16)
    dskip = jnp.repeat(d_skip, HEAD_DIM, axis=1)
    st0 = state_ssm.reshape(depth, dec_batch, D_INNER, D_STATE)

    conv_a_p, conv_s_p, ssm_p, conv_a_s, conv_s_s, ssm_s = [], [], [], [], [], []
    for l in range(depth):
        row = lambda v: v[l].reshape(1, -1)
        h = _rmsnorm(x, norm_w[l], BF16)
        proj = _in_proj(h, w_in[l], W_COL_DT, W_COL_SZ)
        gates = _in_proj(h, w_gate[l], 2 * D_MODEL, 0)
        c, ca_p, ca_s = _conv_a(proj, n_pc, dec_batch, state_conv_a[l], conv_a_w[l],
                                row(conv_a_b), row(ln_a_w), row(ln_a_b))
        yg, cs_p, cs_s, st_p, st_s = _ssd(
            proj, h, n_pc, dec_batch, state_conv_s[l], st0[l], w_dt[l], conv_s_w[l],
            row(conv_s_b), row(dt_bias), row(a_log), row(dskip), row(gnorm_w))
        m = _merge(c, yg, gates, row(b_gate), w_a_out[l], w_b_out[l])
        x = _out_proj(m, w_out[l], x)

        conv_a_p.append(ca_p)
        conv_a_s.append(ca_s)
        conv_s_p.append(cs_p)
        conv_s_s.append(cs_s)
        ssm_p.append(st_p.reshape(batch, N_HEADS, HEAD_DIM, D_STATE))
        ssm_s.append(st_s.reshape(dec_batch, N_HEADS, HEAD_DIM, D_STATE))

    y = _rmsnorm(x, final_norm_w, F32)
    y_prompt = y[:n_p].reshape(batch, seq, D_MODEL)
    y_sample = y[n_p:].reshape(dec_batch, dec_seq, D_MODEL)
    return (y_prompt, y_sample, jnp.stack(conv_a_p), jnp.stack(conv_s_p), jnp.stack(ssm_p),
            jnp.stack(conv_a_s), jnp.stack(conv_s_s), jnp.stack(ssm_s))
```

```python
import functools
import math

import jax
import jax.numpy as jnp
from jax import lax
from jax.experimental import pallas as pl
from jax.experimental.pallas import tpu as pltpu

F32 = jnp.float32
BF16 = jnp.bfloat16

D_MODEL = 2048
C_A = 2048
D_INNER = 4096
N_HEADS = 64
HEAD_DIM = 64
N_GROUPS = 8
D_STATE = 128
CONV_DIM = D_INNER + 2 * N_GROUPS * D_STATE
CONV_A_WIDTH = 31
SSM_CONV_WIDTH = 4
CHUNK = 64
NORM_EPS = 1e-6

COL_SZ = 0
COL_XS = 4096
COL_BC = 8192
COL_AVAL = 10240
COL_AGLU = 12288
COL_AZ = 14336
W_COL_SZ = 3 * C_A
W_COL_DT = W_COL_SZ + D_INNER + CONV_DIM
W_COL_GATE = W_COL_DT + N_HEADS

A_HIST_ROWS = 32
CONV_A_STRIP = 256
S_HIST_ROWS = 8
PACK = 256
GROUP_LANES = D_INNER // N_GROUPS
VMEM_LIMIT = 48 * 1024 * 1024


def _cparams(sem):
    return pltpu.CompilerParams(dimension_semantics=sem, vmem_limit_bytes=VMEM_LIMIT)


def _rmsnorm_kernel(x_ref, w_ref, o_ref):
    x = x_ref[...]
    ms = jnp.mean(x * x, axis=-1, keepdims=True)
    o_ref[...] = (x * lax.rsqrt(ms + NORM_EPS) * w_ref[...]).astype(o_ref.dtype)


def _row_spec(width):
    return pl.BlockSpec((1, width), lambda *_: (0, 0))


def _rmsnorm(x, w_row, out_dtype, tm=512):
    n, d = x.shape
    return pl.pallas_call(
        _rmsnorm_kernel,
        grid=(n // tm,),
        in_specs=[pl.BlockSpec((tm, d), lambda i: (i, 0)), _row_spec(d)],
        out_specs=pl.BlockSpec((tm, d), lambda i: (i, 0)),
        out_shape=jax.ShapeDtypeStruct((n, d), out_dtype),
        compiler_params=_cparams(("parallel",)),
        name="rmsnorm",
    )(x, w_row)


def _final_norm_kernel(x_ref, w_ref, yp_ref, ys_ref, *, n_pt):
    i = pl.program_id(0)
    x = x_ref[...]
    ms = jnp.mean(x * x, axis=-1, keepdims=True)
    y = x * lax.rsqrt(ms + NORM_EPS) * w_ref[...]

    @pl.when(i < n_pt)
    def _():
        yp_ref[...] = y

    @pl.when(i >= n_pt)
    def _():
        ys_ref[...] = y


def _final_norm(x, w3, n_p, tm=512):
    n, d = x.shape
    tm = math.gcd(math.gcd(n_p, n - n_p), tm)
    n_pt = n_p // tm
    return pl.pallas_call(
        functools.partial(_final_norm_kernel, n_pt=n_pt),
        grid=(n // tm,),
        in_specs=[pl.BlockSpec((tm, d), lambda i: (i, 0)), _row_spec(d)],
        out_specs=[pl.BlockSpec((tm, d), lambda i: (jnp.minimum(i, n_pt - 1), 0)),
                   pl.BlockSpec((tm, d), lambda i: (jnp.maximum(i - n_pt, 0), 0))],
        out_shape=[jax.ShapeDtypeStruct((n_p, d), F32),
                   jax.ShapeDtypeStruct((n - n_p, d), F32)],
        compiler_params=_cparams(("arbitrary",)),
        name="final_norm",
    )(x, w3)


def _in_proj_kernel(h_ref, w_ref, o_ref, wbf):
    @pl.when(pl.program_id(1) == 0)
    def _():
        wbf[...] = w_ref[...].astype(BF16)

    o_ref[...] = lax.dot_general(h_ref[...], wbf[...], (((1,), (1,)), ((), ())),
                                 preferred_element_type=F32)


def _in_proj(h, wt, l, n_cols, col_shift, tm=1024, tn=1024):
    n, k = h.shape
    n_w = n_cols // tn
    shift = col_shift // tn
    return pl.pallas_call(
        _in_proj_kernel,
        grid=(n_w, n // tm),
        in_specs=[pl.BlockSpec((tm, k), lambda j, i: (i, 0)),
                  pl.BlockSpec((None, tn, k), lambda j, i: (l, (j + shift) % n_w, 0))],
        out_specs=pl.BlockSpec((tm, tn), lambda j, i: (i, j)),
        out_shape=jax.ShapeDtypeStruct((n, n_cols), F32),
        scratch_shapes=[pltpu.VMEM((tn, k), BF16)],
        compiler_params=_cparams(("arbitrary", "arbitrary")),
        name="in_proj",
    )(h, wt)


def _conv_a_kernel(aval_ref, aglu_ref, az_ref, hist_ref, cw_ref, cb_ref, lnw_ref, lnb_ref,
                   c_ref, stp_ref, sts_ref, ubuf, cbuf, shbuf, *, n_pc):
    i = pl.program_id(0)
    t = CHUNK
    hr = A_HIST_ROWS
    nh = CONV_A_WIDTH - 1

    @pl.when(i == 0)
    def _():
        ubuf[0:hr, :] = jnp.zeros((hr, C_A), F32)

    @pl.when((i > 0) & (i < n_pc))
    def _():
        ubuf[0:hr, :] = ubuf[t:t + hr, :]

    @pl.when(i >= n_pc)
    def _():
        ubuf[hr - nh:hr, :] = hist_ref[0]

    ubuf[hr:hr + t, :] = aval_ref[...] * jax.nn.sigmoid(aglu_ref[...])

    base = hr - nh
    strip = CONV_A_STRIP
    sub = 8
    last_row = base + CONV_A_WIDTH - 1
    for s in range(C_A // strip):
        sl = slice(s * strip, (s + 1) * strip)
        n_q = [(last_row - r) // sub + 1 for r in range(sub)]
        for r in range(1, sub):
            rows = sub * (n_q[r] - 1) + t
            shbuf[r, 0:rows, :] = ubuf[r:r + rows, sl]
        acc = cb_ref[:, sl] + jnp.zeros((t, strip), F32)
        for r in range(sub):
            for q in range(n_q[r]):
                k = sub * q + r - base
                if k >= 0:
                    rows = slice(sub * q, sub * q + t)
                    x = ubuf[rows, sl] if r == 0 else shbuf[r, rows, :]
                    acc = acc + x * cw_ref[k:k + 1, sl]
        cbuf[:, sl] = acc

    c = cbuf[...]
    mu = jnp.mean(c, axis=-1, keepdims=True)
    cc = c - mu
    var = jnp.mean(cc * cc, axis=-1, keepdims=True)
    y = cc * lax.rsqrt(var + NORM_EPS) * lnw_ref[...] + lnb_ref[...]
    z = az_ref[...]
    out = (y * jax.nn.sigmoid(y)) * (z * jax.nn.sigmoid(z))
    c_ref[...] = out.astype(c_ref.dtype)

    @pl.when(i == n_pc - 1)
    def _():
        stp_ref[0] = ubuf[hr + t - nh:hr + t, :]

    @pl.when(i >= n_pc)
    def _():
        sts_ref[0] = ubuf[hr + t - nh:hr + t, :]


def _conv_a(proj, l, n_pc, n_sc, hist, cw, cb, lnw, lnb):
    t = CHUNK
    nh = CONV_A_WIDTH - 1
    n_chunks = n_pc + n_sc
    w2 = C_A
    seq_of = lambda i: (jnp.maximum(i - n_pc, 0), 0, 0)
    kern = functools.partial(_conv_a_kernel, n_pc=n_pc)
    return pl.pallas_call(
        kern,
        grid=(n_chunks,),
        in_specs=[pl.BlockSpec((t, w2), lambda i: (i, COL_AVAL // w2)),
                  pl.BlockSpec((t, w2), lambda i: (i, COL_AGLU // w2)),
                  pl.BlockSpec((t, w2), lambda i: (i, COL_AZ // w2)),
                  pl.BlockSpec((1, nh, C_A), lambda i: (l * n_sc + jnp.maximum(i - n_pc, 0), 0, 0)),
                  pl.BlockSpec((None, CONV_A_WIDTH, C_A), lambda i: (l, 0, 0)),
                  _row_spec(C_A),
                  _row_spec(C_A),
                  _row_spec(C_A)],
        out_specs=[pl.BlockSpec((t, C_A), lambda i: (i, 0)),
                   pl.BlockSpec((1, nh, C_A), lambda i: (0, 0, 0)),
                   pl.BlockSpec((1, nh, C_A), seq_of)],
        out_shape=[jax.ShapeDtypeStruct((n_chunks * t, C_A), BF16),
                   jax.ShapeDtypeStruct((1, nh, C_A), F32),
                   jax.ShapeDtypeStruct((n_sc, nh, C_A), F32)],
        scratch_shapes=[pltpu.VMEM((A_HIST_ROWS + t, C_A), F32),
                        pltpu.VMEM((t, C_A), F32),
                        pltpu.VMEM((8, A_HIST_ROWS + t, CONV_A_STRIP), F32)],
        compiler_params=_cparams(("arbitrary",)),
        name="conv_a",
    )(proj, proj, proj, hist, cw, cb, lnw, lnb)


def _split_bf16(x, parts):
    out = []
    r = x
    for p in range(parts):
        hi = r.astype(BF16)
        out.append(hi)
        if p + 1 < parts:
            r = r - hi.astype(F32)
    return out


def _softplus(x):
    return jnp.maximum(x, 0.0) + jnp.log1p(jnp.exp(-jnp.abs(x)))


def _dt_chain_kernel(h_ref, wdt_ref, dtb_ref, alog_ref, ap_ref, dp_ref):
    tm = h_ref.shape[0]
    dt_raw3 = lax.dot_general(h_ref[...], wdt_ref[...], (((1,), (1,)), ((), ())),
                              preferred_element_type=F32)
    dt3 = _softplus(dt_raw3 + dtb_ref[...])
    a3 = dt3 * (-jnp.exp(alog_ref[...]))
    ri = lax.broadcasted_iota(jnp.int32, (tm, tm), 0)
    ci = lax.broadcasted_iota(jnp.int32, (tm, tm), 1)
    chunk_bits = CHUNK.bit_length() - 1
    assert 1 << chunk_bits == CHUNK
    same_chunk = (ri >> chunk_bits) == (ci >> chunk_bits)
    tri = jnp.where(same_chunk & (ci <= ri), 1.0, 0.0).astype(BF16)
    acum3 = jnp.zeros((tm, 3 * N_HEADS), F32)
    for part in _split_bf16(a3, 3):
        acum3 = acum3 + jnp.dot(tri, part, preferred_element_type=F32)
    lane3 = lax.broadcasted_iota(jnp.int32, (tm, 3 * N_HEADS), 1)
    a_hi, a_mid, a_lo = [p.astype(F32) for p in _split_bf16(acum3, 3)]
    ap_ref[...] = jnp.where(lane3 < N_HEADS, a_hi,
                            jnp.where(lane3 < 2 * N_HEADS, a_mid, a_lo)).astype(BF16)
    lane2 = lax.broadcasted_iota(jnp.int32, (tm, 2 * N_HEADS), 1)
    d_hi, d_lo = [p.astype(F32) for p in _split_bf16(dt3[:, 0:2 * N_HEADS], 2)]
    dp_ref[...] = jnp.where(lane2 < N_HEADS, d_hi, d_lo).astype(BF16)


def _dt_chain(h, wdt, l, dtb, alog, tm=512):
    n, k = h.shape
    return pl.pallas_call(
        _dt_chain_kernel,
        grid=(n // tm,),
        in_specs=[pl.BlockSpec((tm, k), lambda i: (i, 0)),
                  pl.BlockSpec((None, 3 * N_HEADS, k), lambda i: (l, 0, 0)),
                  _row_spec(3 * N_HEADS),
                  _row_spec(3 * N_HEADS)],
        out_specs=[pl.BlockSpec((tm, 3 * N_HEADS), lambda i: (i, 0)),
                   pl.BlockSpec((tm, 2 * N_HEADS), lambda i: (i, 0))],
        out_shape=[jax.ShapeDtypeStruct((n, 3 * N_HEADS), BF16),
                   jax.ShapeDtypeStruct((n, 2 * N_HEADS), BF16)],
        compiler_params=_cparams(("parallel",)),
        name="dt_chain",
    )(h, wdt, dtb, alog)


def _ssd_kernel(sz_ref, xs_ref, bc_ref, ap_s, dp_s, hist_ref, st0_ref, cw_ref, cb_ref,
                dskip_ref, gw_ref,
                yg_ref, cstp_ref, csts_ref, stp_ref, sts_ref,
                xbuf, xs_s, bc_s, st, e_s, ygbuf, *, n_pc):
    i = pl.program_id(0)
    t = CHUNK
    hr = S_HIST_ROWS
    nh = SSM_CONV_WIDTH - 1
    tb = 128

    @pl.when(i == 0)
    def _():
        hh = lax.broadcasted_iota(jnp.int32, (3 * N_HEADS, D_INNER), 0) & (N_HEADS - 1)
        jj = lax.broadcasted_iota(jnp.int32, (3 * N_HEADS, D_INNER), 1)
        e_s[...] = jnp.where((jj >> 6) == hh, 1.0, 0.0).astype(BF16)
        xbuf[0:hr, :] = jnp.zeros((hr, CONV_DIM), F32)
        st[...] = jnp.zeros((D_STATE, D_INNER), F32)

    @pl.when((i > 0) & (i < n_pc))
    def _():
        xbuf[0:hr, :] = xbuf[t:t + hr, :]

    @pl.when(i >= n_pc)
    def _():
        xbuf[hr - nh:hr, :] = hist_ref[0]
        for b in range(D_INNER // tb):
            st[:, b * tb:(b + 1) * tb] = st0_ref[0, b * tb:(b + 1) * tb, :].T

    xbuf[hr:hr + t, 0:D_INNER] = xs_ref[...]
    xbuf[hr:hr + t, D_INNER:CONV_DIM] = bc_ref[...]

    base = hr - nh
    strip = 512

    def conv_strip(s):
        sl = slice(s * strip, (s + 1) * strip)
        acc = jnp.zeros((t, strip), F32)
        for k in range(SSM_CONV_WIDTH):
            acc = acc + xbuf[base + k:base + k + t, sl] * cw_ref[k:k + 1, sl]
        acc = acc + cb_ref[:, sl]
        acc = acc * jax.nn.sigmoid(acc)
        if s * strip < D_INNER:
            xs_s[:, sl] = acc
        else:
            bc_s[:, s * strip - D_INNER:(s + 1) * strip - D_INNER] = acc

    for s in range(CONV_DIM // strip):
        conv_strip(s)

    li = lax.broadcasted_iota(jnp.int32, (t, PACK), 0)
    ji = lax.broadcasted_iota(jnp.int32, (t, PACK), 1)
    si = ji & (HEAD_DIM - 1)
    hi4 = ji >> 6
    diag_mask = li == si
    causal_mask = li >= si

    def group_body(g):
        goff = g * GROUP_LANES
        boff = g * D_STATE
        bm = bc_s[:, pl.ds(boff, D_STATE)].astype(BF16)
        cm = bc_s[:, pl.ds(N_GROUPS * D_STATE + boff, D_STATE)].astype(BF16)
        bm4 = jnp.concatenate([bm, bm, bm, bm], axis=0)
        cb_rep = lax.dot_general(cm, bm4, (((1,), (1,)), ((), ())),
                                 preferred_element_type=F32)
        st_g = st[:, pl.ds(goff, GROUP_LANES)]
        yoff = jnp.dot(cm, st_g.astype(BF16), preferred_element_type=F32)
        ss = jnp.zeros((t, 1), F32)
        for half in range(GROUP_LANES // PACK):
            lanes = pl.ds(goff + half * PACK, PACK)
            acol = jnp.dot(ap_s[...], e_s[:, lanes], preferred_element_type=F32)
            dtx = jnp.dot(dp_s[...], e_s[0:2 * N_HEADS, lanes], preferred_element_type=F32)
            xs = xs_s[:, lanes]
            xd = xs * dtx
            arow = jnp.sum(jnp.where(diag_mask, acol, 0.0), axis=0, keepdims=True)
            lmat = jnp.exp(jnp.where(causal_mask, acol - arow, -1e30))
            gmat = (cb_rep * lmat).astype(BF16)
            bd = jnp.concatenate(
                [jnp.where(hi4 == rr, xd, 0.0).astype(BF16) for rr in range(PACK // HEAD_DIM)],
                axis=0)
            ydiag = jnp.dot(gmat, bd, preferred_element_type=F32)
            y = (ydiag + jnp.exp(acol) * yoff[:, half * PACK:(half + 1) * PACK]
                 + dskip_ref[:, lanes] * xs)
            z = sz_ref[:, lanes]
            yg = y * (z * jax.nn.sigmoid(z))
            ygbuf[:, lanes] = yg
            ss = ss + jnp.sum(yg * yg, axis=-1, keepdims=True)
            alast = acol[t - 1:t, :]
            xdd = (xd * jnp.exp(alast - acol)).astype(BF16)
            upd = lax.dot_general(bm, xdd, (((0,), (0,)), ((), ())),
                                  preferred_element_type=F32)
            st[:, lanes] = st[:, lanes] * jnp.exp(alast) + upd
        scale = lax.rsqrt(ss * (1.0 / GROUP_LANES) + NORM_EPS)
        glanes = pl.ds(goff, GROUP_LANES)
        yg_ref[:, glanes] = (ygbuf[:, glanes] * scale * gw_ref[:, glanes]).astype(yg_ref.dtype)

    for g in range(N_GROUPS):
        group_body(g)

    def write_state(cst_ref, stout_ref):
        cst_ref[0] = xbuf[hr + t - nh:hr + t, :]
        for b in range(D_INNER // tb):
            stout_ref[0, b * tb:(b + 1) * tb, :] = st[:, b * tb:(b + 1) * tb].T

    @pl.when(i == n_pc - 1)
    def _():
        write_state(cstp_ref, stp_ref)

    @pl.when(i >= n_pc)
    def _():
        write_state(csts_ref, sts_ref)


def _ssd(proj, ap, dp, l, n_pc, n_sc, hist, st0, cw, cb, dskip, gw):
    t = CHUNK
    nh = SSM_CONV_WIDTH - 1
    n_chunks = n_pc + n_sc
    seq_of = lambda i: (jnp.maximum(i - n_pc, 0), 0, 0)
    seq_of_l = lambda i: (l * n_sc + jnp.maximum(i - n_pc, 0), 0, 0)
    first = lambda i: (0, 0, 0)
    kern = functools.partial(_ssd_kernel, n_pc=n_pc)
    return pl.pallas_call(
        kern,
        grid=(n_chunks,),
        in_specs=[pl.BlockSpec((t, D_INNER), lambda i: (i, COL_SZ // D_INNER)),
                  pl.BlockSpec((t, D_INNER), lambda i: (i, COL_XS // D_INNER)),
                  pl.BlockSpec((t, 2048), lambda i: (i, COL_BC // 2048)),
                  pl.BlockSpec((t, 3 * N_HEADS), lambda i: (i, 0)),
                  pl.BlockSpec((t, 2 * N_HEADS), lambda i: (i, 0)),
                  pl.BlockSpec((1, nh, CONV_DIM), seq_of_l),
                  pl.BlockSpec((1, D_INNER, D_STATE), seq_of_l),
                  pl.BlockSpec((None, SSM_CONV_WIDTH, CONV_DIM), lambda i: (l, 0, 0)),
                  _row_spec(CONV_DIM),
                  _row_spec(D_INNER),
                  _row_spec(D_INNER)],
        out_specs=[pl.BlockSpec((t, D_INNER), lambda i: (i, 0)),
                   pl.BlockSpec((1, nh, CONV_DIM), first),
                   pl.BlockSpec((1, nh, CONV_DIM), seq_of),
                   pl.BlockSpec((1, D_INNER, D_STATE), first),
                   pl.BlockSpec((1, D_INNER, D_STATE), seq_of)],
        out_shape=[jax.ShapeDtypeStruct((n_chunks * t, D_INNER), BF16),
                   jax.ShapeDtypeStruct((1, nh, CONV_DIM), F32),
                   jax.ShapeDtypeStruct((n_sc, nh, CONV_DIM), F32),
                   jax.ShapeDtypeStruct((1, D_INNER, D_STATE), F32),
                   jax.ShapeDtypeStruct((n_sc, D_INNER, D_STATE), F32)],
        scratch_shapes=[pltpu.VMEM((S_HIST_ROWS + t, CONV_DIM), F32),
                        pltpu.VMEM((t, D_INNER), F32),
                        pltpu.VMEM((t, 2 * N_GROUPS * D_STATE), F32),
                        pltpu.VMEM((D_STATE, D_INNER), F32),
                        pltpu.VMEM((3 * N_HEADS, D_INNER), BF16),
                        pltpu.VMEM((t, D_INNER), F32)],
        compiler_params=_cparams(("arbitrary",)),
        name="ssd",
    )(proj, proj, proj, ap, dp, hist, st0, cw, cb, dskip, gw)


def _merge_kernel(c_ref, yg_ref, ga_ref, gb_ref, bga_ref, bgb_ref, wa_ref, wb_ref, m_ref,
                  wabf, wbbf):
    @pl.when(pl.program_id(1) == 0)
    def _():
        wabf[...] = wa_ref[...].astype(BF16)
        wbbf[...] = wb_ref[...].astype(BF16)

    a_out = jnp.dot(c_ref[...], wabf[...], preferred_element_type=F32)
    b_out = jnp.dot(yg_ref[...], wbbf[...], preferred_element_type=F32)
    ga = jax.nn.sigmoid(ga_ref[...] + bga_ref[...])
    gb = jax.nn.sigmoid(gb_ref[...] + bgb_ref[...])
    m_ref[...] = (ga * a_out + gb * b_out).astype(m_ref.dtype)


def _merge(c, yg, gates, l, bg3, wa, wb, tm=256, tn=512):
    n = c.shape[0]
    nj = D_MODEL // tn
    return pl.pallas_call(
        _merge_kernel,
        grid=(nj, n // tm),
        in_specs=[pl.BlockSpec((tm, C_A), lambda j, i: (i, 0)),
                  pl.BlockSpec((tm, D_INNER), lambda j, i: (i, 0)),
                  pl.BlockSpec((tm, tn), lambda j, i: (i, j)),
                  pl.BlockSpec((tm, tn), lambda j, i: (i, nj + j)),
                  pl.BlockSpec((1, tn), lambda j, i: (0, j)),
                  pl.BlockSpec((1, tn), lambda j, i: (0, nj + j)),
                  pl.BlockSpec((None, C_A, tn), lambda j, i: (l, 0, j)),
                  pl.BlockSpec((None, D_INNER, tn), lambda j, i: (l, 0, j))],
        out_specs=pl.BlockSpec((tm, tn), lambda j, i: (i, j)),
        out_shape=jax.ShapeDtypeStruct((n, D_MODEL), BF16),
        scratch_shapes=[pltpu.VMEM((C_A, tn), BF16), pltpu.VMEM((D_INNER, tn), BF16)],
        compiler_params=_cparams(("arbitrary", "arbitrary")),
        name="merge",
    )(c, yg, gates, gates, bg3, bg3, wa, wb)


def _out_kernel(m_ref, w_ref, x_ref, o_ref, wbf):
    @pl.when(pl.program_id(1) == 0)
    def _():
        wbf[...] = w_ref[...].astype(BF16)

    o_ref[...] = x_ref[...] + jnp.dot(m_ref[...], wbf[...], preferred_element_type=F32)


def _out_proj(m, w, l, x, tm=512, tn=1024):
    n = m.shape[0]
    return pl.pallas_call(
        _out_kernel,
        grid=(D_MODEL // tn, n // tm),
        in_specs=[pl.BlockSpec((tm, D_MODEL), lambda j, i: (i, 0)),
                  pl.BlockSpec((None, D_MODEL, tn), lambda j, i: (l, 0, j)),
                  pl.BlockSpec((tm, tn), lambda j, i: (i, j))],
        out_specs=pl.BlockSpec((tm, tn), lambda j, i: (i, j)),
        out_shape=jax.ShapeDtypeStruct((n, D_MODEL), F32),
        scratch_shapes=[pltpu.VMEM((D_MODEL, tn), BF16)],
        compiler_params=_cparams(("arbitrary", "arbitrary")),
        name="out_proj",
    )(m, w, x)


def kernel(x_prompt, x_sample, state_conv_a, state_conv_s, state_ssm, norm_w, w_in, b_gate,
           conv_a_w, conv_a_b, ln_a_w, ln_a_b, w_a_out, conv_s_w, conv_s_b, dt_bias, a_log,
           d_skip, gnorm_w, w_b_out, w_out, final_norm_w):
    depth = w_in.shape[0]
    batch, seq, _ = x_prompt.shape
    dec_batch, dec_seq, _ = x_sample.shape
    assert batch == 1 and dec_seq == CHUNK and seq % CHUNK == 0
    assert w_in.shape[2] == W_COL_GATE + 2 * D_MODEL
    n_p = batch * seq
    n_s = dec_batch * dec_seq
    n_pc = n_p // CHUNK

    x = jnp.concatenate([x_prompt.reshape(n_p, D_MODEL), x_sample.reshape(n_s, D_MODEL)], axis=0)
    w_in_t = jnp.swapaxes(w_in, 1, 2)
    w_gate_t = w_in_t[:, W_COL_GATE:, :]
    w_dt_t = jnp.tile(w_in_t[:, W_COL_DT:W_COL_GATE, :], (1, 3, 1)).astype(BF16)
    dtb3 = jnp.tile(dt_bias, (1, 3))
    alog3 = jnp.tile(a_log, (1, 3))
    dskip = jnp.repeat(d_skip, HEAD_DIM, axis=1)
    st0 = state_ssm.reshape(depth * dec_batch, D_INNER, D_STATE)
    hist_a = state_conv_a.reshape(depth * dec_batch, CONV_A_WIDTH - 1, C_A)
    hist_s = state_conv_s.reshape(depth * dec_batch, SSM_CONV_WIDTH - 1, CONV_DIM)

    conv_a_p, conv_s_p, ssm_p, conv_a_s, conv_s_s, ssm_s = [], [], [], [], [], []
    for l in range(depth):
        row = lambda v: v[l].reshape(1, -1)
        h = _rmsnorm(x, row(norm_w), BF16)
        proj = _in_proj(h, w_in_t, l, W_COL_DT, W_COL_SZ)
        gates = _in_proj(h, w_gate_t, l, 2 * D_MODEL, 0)
        c, ca_p, ca_s = _conv_a(proj, l, n_pc, dec_batch, hist_a, conv_a_w,
                                row(conv_a_b), row(ln_a_w), row(ln_a_b))
        ap, dp = _dt_chain(h, w_dt_t, l, row(dtb3), row(alog3))
        yg, cs_p, cs_s, st_p, st_s = _ssd(
            proj, ap, dp, l, n_pc, dec_batch, hist_s, st0, conv_s_w,
            row(conv_s_b), row(dskip), row(gnorm_w))
        m = _merge(c, yg, gates, l, row(b_gate), w_a_out, w_b_out)
        x = _out_proj(m, w_out, l, x)

        conv_a_p.append(ca_p)
        conv_a_s.append(ca_s)
        conv_s_p.append(cs_p)
        conv_s_s.append(cs_s)
        ssm_p.append(st_p.reshape(batch, N_HEADS, HEAD_DIM, D_STATE))
        ssm_s.append(st_s.reshape(dec_batch, N_HEADS, HEAD_DIM, D_STATE))

    y_p, y_s = _final_norm(x, final_norm_w.reshape(1, D_MODEL), n_p)
    y_prompt = y_p.reshape(batch, seq, D_MODEL)
    y_sample = y_s.reshape(dec_batch, dec_seq, D_MODEL)
    return (y_prompt, y_sample, jnp.stack(conv_a_p), jnp.stack(conv_s_p), jnp.stack(ssm_p),
            jnp.stack(conv_a_s), jnp.stack(conv_s_s), jnp.stack(ssm_s))
```

```python
import functools
import math

import jax
import jax.numpy as jnp
from jax import lax
from jax.experimental import pallas as pl
from jax.experimental.pallas import tpu as pltpu

F32 = jnp.float32
BF16 = jnp.bfloat16

D_MODEL = 2048
C_A = 2048
D_INNER = 4096
N_HEADS = 64
HEAD_DIM = 64
N_GROUPS = 8
D_STATE = 128
CONV_DIM = D_INNER + 2 * N_GROUPS * D_STATE
CONV_A_WIDTH = 31
SSM_CONV_WIDTH = 4
CHUNK = 64
NORM_EPS = 1e-6

COL_AVAL = 0
COL_AGLU = 2048
COL_AZ = 4096
COL_SZ = 0
COL_XS = 4096
COL_BC = 8192
W_COL_SZ = 3 * C_A
W_COL_DT = W_COL_SZ + D_INNER + CONV_DIM
W_COL_GATE = W_COL_DT + N_HEADS

A_HIST_ROWS = 32
CONV_A_STRIP = 256
S_HIST_ROWS = 8
PACK = 256
GROUP_LANES = D_INNER // N_GROUPS
VMEM_LIMIT = 56 * 1024 * 1024


def _cparams(sem):
    return pltpu.CompilerParams(dimension_semantics=sem, vmem_limit_bytes=VMEM_LIMIT)


def _rmsnorm_kernel(x_ref, w_ref, o_ref):
    x = x_ref[...]
    ms = jnp.mean(x * x, axis=-1, keepdims=True)
    o_ref[...] = (x * lax.rsqrt(ms + NORM_EPS) * w_ref[...]).astype(o_ref.dtype)


def _row_spec(width):
    return pl.BlockSpec((1, width), lambda *_: (0, 0))


def _rmsnorm(x, w_row, out_dtype, tm=512):
    n, d = x.shape
    return pl.pallas_call(
        _rmsnorm_kernel,
        grid=(n // tm,),
        in_specs=[pl.BlockSpec((tm, d), lambda i: (i, 0)), _row_spec(d)],
        out_specs=pl.BlockSpec((tm, d), lambda i: (i, 0)),
        out_shape=jax.ShapeDtypeStruct((n, d), out_dtype),
        compiler_params=_cparams(("parallel",)),
        name="rmsnorm",
    )(x, w_row)


def _final_norm_kernel(x_ref, w_ref, yp_ref, ys_ref, *, n_pt):
    i = pl.program_id(0)
    x = x_ref[...]
    ms = jnp.mean(x * x, axis=-1, keepdims=True)
    y = x * lax.rsqrt(ms + NORM_EPS) * w_ref[...]

    @pl.when(i < n_pt)
    def _():
        yp_ref[...] = y

    @pl.when(i >= n_pt)
    def _():
        ys_ref[...] = y


def _final_norm(x, w3, n_p, tm=512):
    n, d = x.shape
    tm = math.gcd(math.gcd(n_p, n - n_p), tm)
    n_pt = n_p // tm
    return pl.pallas_call(
        functools.partial(_final_norm_kernel, n_pt=n_pt),
        grid=(n // tm,),
        in_specs=[pl.BlockSpec((tm, d), lambda i: (i, 0)), _row_spec(d)],
        out_specs=[pl.BlockSpec((tm, d), lambda i: (jnp.minimum(i, n_pt - 1), 0)),
                   pl.BlockSpec((tm, d), lambda i: (jnp.maximum(i - n_pt, 0), 0))],
        out_shape=[jax.ShapeDtypeStruct((n_p, d), F32),
                   jax.ShapeDtypeStruct((n - n_p, d), F32)],
        compiler_params=_cparams(("arbitrary",)),
        name="final_norm",
    )(x, w3)


def _in_proj_kernel(h_ref, w_ref, o_ref, wbf):
    @pl.when(pl.program_id(1) == 0)
    def _():
        wbf[...] = w_ref[...].astype(BF16)

    o_ref[...] = lax.dot_general(h_ref[...], wbf[...], (((1,), (1,)), ((), ())),
                                 preferred_element_type=F32)


def _in_proj(h, wt, l, col0, n_cols, tm=1024, tn=1024):
    n, k = h.shape
    n_w = n_cols // tn
    t0 = col0 // tn
    return pl.pallas_call(
        _in_proj_kernel,
        grid=(n_w, n // tm),
        in_specs=[pl.BlockSpec((tm, k), lambda j, i: (i, 0)),
                  pl.BlockSpec((None, tn, k), lambda j, i: (l, t0 + j, 0))],
        out_specs=pl.BlockSpec((tm, tn), lambda j, i: (i, j)),
        out_shape=jax.ShapeDtypeStruct((n, n_cols), F32),
        scratch_shapes=[pltpu.VMEM((tn, k), BF16)],
        compiler_params=_cparams(("arbitrary", "arbitrary")),
        name="in_proj",
    )(h, wt)


def _conv_a_kernel(aval_ref, aglu_ref, az_ref, hist_ref, cw_ref, cb_ref, lnw_ref, lnb_ref,
                   c_ref, stp_ref, sts_ref, ubuf, cbuf, shbuf, *, n_pc):
    g = pl.program_id(0)
    t = CHUNK
    hr = A_HIST_ROWS
    nh = CONV_A_WIDTH - 1

    @pl.when(g == 0)
    def _():
        ubuf[...] = jnp.zeros(ubuf.shape, F32)

    ubuf[0:hr, :] = ubuf[t:t + hr, :]
    ubuf[hr - nh:hr, :] = jnp.where(g >= n_pc, hist_ref[0], ubuf[hr - nh:hr, :])
    ubuf[hr:hr + t, :] = aval_ref[...] * jax.nn.sigmoid(aglu_ref[...])

    base = hr - nh
    strip = CONV_A_STRIP
    sub = 8
    last_row = base + CONV_A_WIDTH - 1
    for s in range(C_A // strip):
        sl = slice(s * strip, (s + 1) * strip)
        n_q = [(last_row - r) // sub + 1 for r in range(sub)]
        for r in range(1, sub):
            rows = sub * (n_q[r] - 1) + t
            shbuf[r, 0:rows, :] = ubuf[r:r + rows, sl]
        acc = cb_ref[:, sl] + jnp.zeros((t, strip), F32)
        for r in range(sub):
            for q in range(n_q[r]):
                k = sub * q + r - base
                if k >= 0:
                    rows = slice(sub * q, sub * q + t)
                    x = ubuf[rows, sl] if r == 0 else shbuf[r, rows, :]
                    acc = acc + x * cw_ref[k:k + 1, sl]
        cbuf[:, sl] = acc

    c = cbuf[...]
    mu = jnp.mean(c, axis=-1, keepdims=True)
    cc = c - mu
    var = jnp.mean(cc * cc, axis=-1, keepdims=True)
    y = cc * lax.rsqrt(var + NORM_EPS) * lnw_ref[...] + lnb_ref[...]
    z = az_ref[...]
    out = (y * jax.nn.sigmoid(y)) * (z * jax.nn.sigmoid(z))
    c_ref[...] = out.astype(c_ref.dtype)

    sts_ref[0] = ubuf[hr + t - nh:hr + t, :]

    @pl.when(g == n_pc - 1)
    def _():
        stp_ref[0] = ubuf[hr + t - nh:hr + t, :]


def _conv_a(proj_a, l, n_pc, n_sc, hist, cw, cb, lnw, lnb):
    t = CHUNK
    nh = CONV_A_WIDTH - 1
    n_chunks = n_pc + n_sc
    w2 = C_A
    stream_of = lambda i: jnp.maximum(i - n_pc, 0)
    kern = functools.partial(_conv_a_kernel, n_pc=n_pc)
    return pl.pallas_call(
        kern,
        grid=(n_chunks,),
        in_specs=[pl.BlockSpec((t, w2), lambda i: (i, COL_AVAL // w2)),
                  pl.BlockSpec((t, w2), lambda i: (i, COL_AGLU // w2)),
                  pl.BlockSpec((t, w2), lambda i: (i, COL_AZ // w2)),
                  pl.BlockSpec((1, nh, C_A), lambda i: (l * n_sc + stream_of(i), 0, 0)),
                  pl.BlockSpec((None, CONV_A_WIDTH, C_A), lambda i: (l, 0, 0)),
                  _row_spec(C_A),
                  _row_spec(C_A),
                  _row_spec(C_A)],
        out_specs=[pl.BlockSpec((t, C_A), lambda i: (i, 0)),
                   pl.BlockSpec((1, nh, C_A), lambda i: (0, 0, 0)),
                   pl.BlockSpec((1, nh, C_A), lambda i: (stream_of(i), 0, 0))],
        out_shape=[jax.ShapeDtypeStruct((n_chunks * t, C_A), BF16),
                   jax.ShapeDtypeStruct((1, nh, C_A), F32),
                   jax.ShapeDtypeStruct((n_sc, nh, C_A), F32)],
        scratch_shapes=[pltpu.VMEM((A_HIST_ROWS + t, C_A), F32),
                        pltpu.VMEM((t, C_A), F32),
                        pltpu.VMEM((8, A_HIST_ROWS + t, CONV_A_STRIP), F32)],
        compiler_params=_cparams(("arbitrary",)),
        name="conv_a",
    )(proj_a, proj_a, proj_a, hist, cw, cb, lnw, lnb)


def _split_bf16(x, parts):
    out = []
    r = x
    for p in range(parts):
        hi = r.astype(BF16)
        out.append(hi)
        if p + 1 < parts:
            r = r - hi.astype(F32)
    return out


def _softplus(x):
    return jnp.maximum(x, 0.0) + jnp.log1p(jnp.exp(-jnp.abs(x)))


def _dt_chain_kernel(h_ref, wdt_ref, dtb_ref, alog_ref, ap_ref, dp_ref):
    tm = h_ref.shape[0]
    dt_raw3 = lax.dot_general(h_ref[...], wdt_ref[...], (((1,), (1,)), ((), ())),
                              preferred_element_type=F32)
    dt3 = _softplus(dt_raw3 + dtb_ref[...])
    a3 = dt3 * (-jnp.exp(alog_ref[...]))
    ri = lax.broadcasted_iota(jnp.int32, (tm, tm), 0)
    ci = lax.broadcasted_iota(jnp.int32, (tm, tm), 1)
    chunk_bits = CHUNK.bit_length() - 1
    assert 1 << chunk_bits == CHUNK
    same_chunk = (ri >> chunk_bits) == (ci >> chunk_bits)
    tri = jnp.where(same_chunk & (ci <= ri), 1.0, 0.0).astype(BF16)
    acum3 = jnp.zeros((tm, 3 * N_HEADS), F32)
    for part in _split_bf16(a3, 3):
        acum3 = acum3 + jnp.dot(tri, part, preferred_element_type=F32)
    lane3 = lax.broadcasted_iota(jnp.int32, (tm, 3 * N_HEADS), 1)
    a_hi, a_mid, a_lo = [p.astype(F32) for p in _split_bf16(acum3, 3)]
    ap_ref[...] = jnp.where(lane3 < N_HEADS, a_hi,
                            jnp.where(lane3 < 2 * N_HEADS, a_mid, a_lo)).astype(BF16)
    lane2 = lax.broadcasted_iota(jnp.int32, (tm, 2 * N_HEADS), 1)
    d_hi, d_lo = [p.astype(F32) for p in _split_bf16(dt3[:, 0:2 * N_HEADS], 2)]
    dp_ref[...] = jnp.where(lane2 < N_HEADS, d_hi, d_lo).astype(BF16)


def _dt_chain(h, wdt, l, dtb, alog, tm=512):
    n, k = h.shape
    return pl.pallas_call(
        _dt_chain_kernel,
        grid=(n // tm,),
        in_specs=[pl.BlockSpec((tm, k), lambda i: (i, 0)),
                  pl.BlockSpec((None, 3 * N_HEADS, k), lambda i: (l, 0, 0)),
                  _row_spec(3 * N_HEADS),
                  _row_spec(3 * N_HEADS)],
        out_specs=[pl.BlockSpec((tm, 3 * N_HEADS), lambda i: (i, 0)),
                   pl.BlockSpec((tm, 2 * N_HEADS), lambda i: (i, 0))],
        out_shape=[jax.ShapeDtypeStruct((n, 3 * N_HEADS), BF16),
                   jax.ShapeDtypeStruct((n, 2 * N_HEADS), BF16)],
        compiler_params=_cparams(("parallel",)),
        name="dt_chain",
    )(h, wdt, dtb, alog)


def _ssd_kernel(sz_ref, xs_ref, bc_ref, ap_s, dp_s, hist_ref, st0_ref, cw_ref, cb_ref,
                dskip_ref, gw_ref,
                yg_ref, cstp_ref, csts_ref, stp_ref, sts_ref,
                xbuf, xs_s, bc_s, st, e_s, ygbuf, *, n_pc):
    i = pl.program_id(0)
    t = CHUNK
    hr = S_HIST_ROWS
    nh = SSM_CONV_WIDTH - 1
    tb = 128

    @pl.when(i == 0)
    def _():
        hh = lax.broadcasted_iota(jnp.int32, (3 * N_HEADS, D_INNER), 0) & (N_HEADS - 1)
        jj = lax.broadcasted_iota(jnp.int32, (3 * N_HEADS, D_INNER), 1)
        e_s[...] = jnp.where((jj >> 6) == hh, 1.0, 0.0).astype(BF16)
        xbuf[0:hr, :] = jnp.zeros((hr, CONV_DIM), F32)
        st[...] = jnp.zeros((D_STATE, D_INNER), F32)

    @pl.when((i > 0) & (i < n_pc))
    def _():
        xbuf[0:hr, :] = xbuf[t:t + hr, :]

    @pl.when(i >= n_pc)
    def _():
        xbuf[hr - nh:hr, :] = hist_ref[0]
        for b in range(D_INNER // tb):
            st[:, b * tb:(b + 1) * tb] = st0_ref[0, b * tb:(b + 1) * tb, :].T

    xbuf[hr:hr + t, 0:D_INNER] = xs_ref[...]
    xbuf[hr:hr + t, D_INNER:CONV_DIM] = bc_ref[...]

    base = hr - nh
    strip = 512

    def conv_strip(s):
        sl = slice(s * strip, (s + 1) * strip)
        acc = jnp.zeros((t, strip), F32)
        for k in range(SSM_CONV_WIDTH):
            acc = acc + xbuf[base + k:base + k + t, sl] * cw_ref[k:k + 1, sl]
        acc = acc + cb_ref[:, sl]
        acc = acc * jax.nn.sigmoid(acc)
        if s * strip < D_INNER:
            xs_s[:, sl] = acc
        else:
            bc_s[:, s * strip - D_INNER:(s + 1) * strip - D_INNER] = acc

    for s in range(CONV_DIM // strip):
        conv_strip(s)

    li = lax.broadcasted_iota(jnp.int32, (t, PACK), 0)
    ji = lax.broadcasted_iota(jnp.int32, (t, PACK), 1)
    si = ji & (HEAD_DIM - 1)
    hi4 = ji >> 6
    diag_mask = li == si
    causal_mask = li >= si

    def group_body(g):
        goff = g * GROUP_LANES
        boff = g * D_STATE
        bm = bc_s[:, pl.ds(boff, D_STATE)].astype(BF16)
        cm = bc_s[:, pl.ds(N_GROUPS * D_STATE + boff, D_STATE)].astype(BF16)
        bm4 = jnp.concatenate([bm, bm, bm, bm], axis=0)
        cb_rep = lax.dot_general(cm, bm4, (((1,), (1,)), ((), ())),
                                 preferred_element_type=F32)
        st_g = st[:, pl.ds(goff, GROUP_LANES)]
        yoff = jnp.dot(cm, st_g.astype(BF16), preferred_element_type=F32)
        ss = jnp.zeros((t, 1), F32)
        for half in range(GROUP_LANES // PACK):
            lanes = pl.ds(goff + half * PACK, PACK)
            acol = jnp.dot(ap_s[...], e_s[:, lanes], preferred_element_type=F32)
            dtx = jnp.dot(dp_s[...], e_s[0:2 * N_HEADS, lanes], preferred_element_type=F32)
            xs = xs_s[:, lanes]
            xd = xs * dtx
            arow = jnp.sum(jnp.where(diag_mask, acol, 0.0), axis=0, keepdims=True)
            lmat = jnp.exp(jnp.where(causal_mask, acol - arow, -1e30))
            gmat = (cb_rep * lmat).astype(BF16)
            bd = jnp.concatenate(
                [jnp.where(hi4 == rr, xd, 0.0).astype(BF16) for rr in range(PACK // HEAD_DIM)],
                axis=0)
            ydiag = jnp.dot(gmat, bd, preferred_element_type=F32)
            y = (ydiag + jnp.exp(acol) * yoff[:, half * PACK:(half + 1) * PACK]
                 + dskip_ref[:, lanes] * xs)
            z = sz_ref[:, lanes]
            yg = y * (z * jax.nn.sigmoid(z))
            ygbuf[:, lanes] = yg
            ss = ss + jnp.sum(yg * yg, axis=-1, keepdims=True)
            alast = acol[t - 1:t, :]
            xdd = (xd * jnp.exp(alast - acol)).astype(BF16)
            upd = lax.dot_general(bm, xdd, (((0,), (0,)), ((), ())),
                                  preferred_element_type=F32)
            st[:, lanes] = st[:, lanes] * jnp.exp(alast) + upd
        scale = lax.rsqrt(ss * (1.0 / GROUP_LANES) + NORM_EPS)
        glanes = pl.ds(goff, GROUP_LANES)
        yg_ref[:, glanes] = (ygbuf[:, glanes] * scale * gw_ref[:, glanes]).astype(yg_ref.dtype)

    for g in range(N_GROUPS):
        group_body(g)

    def write_state(cst_ref, stout_ref):
        cst_ref[0] = xbuf[hr + t - nh:hr + t, :]
        for b in range(D_INNER // tb):
            stout_ref[0, b * tb:(b + 1) * tb, :] = st[:, b * tb:(b + 1) * tb].T

    @pl.when(i == n_pc - 1)
    def _():
        write_state(cstp_ref, stp_ref)

    @pl.when(i >= n_pc)
    def _():
        write_state(csts_ref, sts_ref)


def _ssd(proj, ap, dp, l, n_pc, n_sc, hist, st0, cw, cb, dskip, gw):
    t = CHUNK
    nh = SSM_CONV_WIDTH - 1
    n_chunks = n_pc + n_sc
    seq_of = lambda i: (jnp.maximum(i - n_pc, 0), 0, 0)
    seq_of_l = lambda i: (l * n_sc + jnp.maximum(i - n_pc, 0), 0, 0)
    first = lambda i: (0, 0, 0)
    kern = functools.partial(_ssd_kernel, n_pc=n_pc)
    return pl.pallas_call(
        kern,
        grid=(n_chunks,),
        in_specs=[pl.BlockSpec((t, D_INNER), lambda i: (i, COL_SZ // D_INNER)),
                  pl.BlockSpec((t, D_INNER), lambda i: (i, COL_XS // D_INNER)),
                  pl.BlockSpec((t, 2048), lambda i: (i, COL_BC // 2048)),
                  pl.BlockSpec((t, 3 * N_HEADS), lambda i: (i, 0)),
                  pl.BlockSpec((t, 2 * N_HEADS), lambda i: (i, 0)),
                  pl.BlockSpec((1, nh, CONV_DIM), seq_of_l),
                  pl.BlockSpec((1, D_INNER, D_STATE), seq_of_l),
                  pl.BlockSpec((None, SSM_CONV_WIDTH, CONV_DIM), lambda i: (l, 0, 0)),
                  _row_spec(CONV_DIM),
                  _row_spec(D_INNER),
                  _row_spec(D_INNER)],
        out_specs=[pl.BlockSpec((t, D_INNER), lambda i: (i, 0)),
                   pl.BlockSpec((1, nh, CONV_DIM), first),
                   pl.BlockSpec((1, nh, CONV_DIM), seq_of),
                   pl.BlockSpec((1, D_INNER, D_STATE), first),
                   pl.BlockSpec((1, D_INNER, D_STATE), seq_of)],
        out_shape=[jax.ShapeDtypeStruct((n_chunks * t, D_INNER), BF16),
                   jax.ShapeDtypeStruct((1, nh, CONV_DIM), F32),
                   jax.ShapeDtypeStruct((n_sc, nh, CONV_DIM), F32),
                   jax.ShapeDtypeStruct((1, D_INNER, D_STATE), F32),
                   jax.ShapeDtypeStruct((n_sc, D_INNER, D_STATE), F32)],
        scratch_shapes=[pltpu.VMEM((S_HIST_ROWS + t, CONV_DIM), F32),
                        pltpu.VMEM((t, D_INNER), F32),
                        pltpu.VMEM((t, 2 * N_GROUPS * D_STATE), F32),
                        pltpu.VMEM((D_STATE, D_INNER), F32),
                        pltpu.VMEM((3 * N_HEADS, D_INNER), BF16),
                        pltpu.VMEM((t, D_INNER), F32)],
        compiler_params=_cparams(("arbitrary",)),
        name="ssd",
    )(proj, proj, proj, ap, dp, hist, st0, cw, cb, dskip, gw)


def _merge_kernel(c_ref, yg_ref, ga_ref, gb_ref, bga_ref, bgb_ref, wa_ref, wb_ref, m_ref):
    a_out = jnp.dot(c_ref[...], wa_ref[...], preferred_element_type=F32)
    b_out = jnp.dot(yg_ref[...], wb_ref[...], preferred_element_type=F32)
    ga = jax.nn.sigmoid(ga_ref[...] + bga_ref[...])
    gb = jax.nn.sigmoid(gb_ref[...] + bgb_ref[...])
    m_ref[...] = (ga * a_out + gb * b_out).astype(m_ref.dtype)


def _merge(c, yg, gates, l, bg3, wa, wb, tm=512, tn=1024):
    n = c.shape[0]
    nj = D_MODEL // tn
    return pl.pallas_call(
        _merge_kernel,
        grid=(nj, n // tm),
        in_specs=[pl.BlockSpec((tm, C_A), lambda j, i: (i, 0)),
                  pl.BlockSpec((tm, D_INNER), lambda j, i: (i, 0)),
                  pl.BlockSpec((tm, tn), lambda j, i: (i, j)),
                  pl.BlockSpec((tm, tn), lambda j, i: (i, nj + j)),
                  pl.BlockSpec((1, tn), lambda j, i: (0, j)),
                  pl.BlockSpec((1, tn), lambda j, i: (0, nj + j)),
                  pl.BlockSpec((None, C_A, tn), lambda j, i: (l, 0, j)),
                  pl.BlockSpec((None, D_INNER, tn), lambda j, i: (l, 0, j))],
        out_specs=pl.BlockSpec((tm, tn), lambda j, i: (i, j)),
        out_shape=jax.ShapeDtypeStruct((n, D_MODEL), BF16),
        compiler_params=_cparams(("parallel", "parallel")),
        name="merge",
    )(c, yg, gates, gates, bg3, bg3, wa, wb)


def _out_kernel(m_ref, w_ref, x_ref, o_ref):
    o_ref[...] = x_ref[...] + jnp.dot(m_ref[...], w_ref[...], preferred_element_type=F32)


def _out_norm_kernel(m_ref, w_ref, x_ref, nw_ref, o_ref, h_ref):
    x = x_ref[...] + jnp.dot(m_ref[...], w_ref[...], preferred_element_type=F32)
    o_ref[...] = x
    ms = jnp.mean(x * x, axis=-1, keepdims=True)
    h_ref[...] = (x * lax.rsqrt(ms + NORM_EPS) * nw_ref[...]).astype(h_ref.dtype)


def _out_proj(m, w, l, x, next_norm_w=None, tm=512):
    n = m.shape[0]
    tile = pl.BlockSpec((tm, D_MODEL), lambda i: (i, 0))
    w_spec = pl.BlockSpec((None, D_MODEL, D_MODEL), lambda i: (l, 0, 0))
    x_shape = jax.ShapeDtypeStruct((n, D_MODEL), F32)
    if next_norm_w is None:
        return pl.pallas_call(
            _out_kernel, grid=(n // tm,), in_specs=[tile, w_spec, tile], out_specs=tile,
            out_shape=x_shape, compiler_params=_cparams(("parallel",)), name="out_proj",
        )(m, w, x), None
    return pl.pallas_call(
        _out_norm_kernel, grid=(n // tm,),
        in_specs=[tile, w_spec, tile, _row_spec(D_MODEL)], out_specs=[tile, tile],
        out_shape=[x_shape, jax.ShapeDtypeStruct((n, D_MODEL), BF16)],
        compiler_params=_cparams(("parallel",)), name="out_proj_norm",
    )(m, w, x, next_norm_w)


def kernel(x_prompt, x_sample, state_conv_a, state_conv_s, state_ssm, norm_w, w_in, b_gate,
           conv_a_w, conv_a_b, ln_a_w, ln_a_b, w_a_out, conv_s_w, conv_s_b, dt_bias, a_log,
           d_skip, gnorm_w, w_b_out, w_out, final_norm_w):
    depth = w_in.shape[0]
    batch, seq, _ = x_prompt.shape
    dec_batch, dec_seq, _ = x_sample.shape
    assert batch == 1 and dec_seq == CHUNK and seq % CHUNK == 0
    assert w_in.shape[2] == W_COL_GATE + 2 * D_MODEL
    n_p = batch * seq
    n_s = dec_batch * dec_seq
    n_pc = n_p // CHUNK

    x = jnp.concatenate([x_prompt.reshape(n_p, D_MODEL), x_sample.reshape(n_s, D_MODEL)], axis=0)
    w_in_t = jnp.swapaxes(w_in, 1, 2)
    w_gate_t = w_in_t[:, W_COL_GATE:, :]
    w_dt_t = jnp.tile(w_in_t[:, W_COL_DT:W_COL_GATE, :], (1, 3, 1)).astype(BF16)
    dtb3 = jnp.tile(dt_bias, (1, 3))
    alog3 = jnp.tile(a_log, (1, 3))
    dskip = jnp.repeat(d_skip, HEAD_DIM, axis=1)
    st0 = state_ssm.reshape(depth * dec_batch, D_INNER, D_STATE)
    hist_a = state_conv_a.reshape(depth * dec_batch, CONV_A_WIDTH - 1, C_A)
    hist_s = state_conv_s.reshape(depth * dec_batch, SSM_CONV_WIDTH - 1, CONV_DIM)

    wa_bf, wb_bf, wo_bf = w_a_out.astype(BF16), w_b_out.astype(BF16), w_out.astype(BF16)

    conv_a_p, conv_s_p, ssm_p, conv_a_s, conv_s_s, ssm_s = [], [], [], [], [], []
    h = _rmsnorm(x, norm_w[0].reshape(1, -1), BF16)
    for l in range(depth):
        row = lambda v: v[l].reshape(1, -1)
        proj_a = _in_proj(h, w_in_t, l, 0, W_COL_SZ)
        proj_b = _in_proj(h, w_in_t, l, W_COL_SZ, W_COL_DT - W_COL_SZ)
        gates = _in_proj(h, w_gate_t, l, 0, 2 * D_MODEL)
        ap, dp = _dt_chain(h, w_dt_t, l, row(dtb3), row(alog3))
        c, ca_p, ca_s = _conv_a(proj_a, l, n_pc, dec_batch, hist_a, conv_a_w,
                                row(conv_a_b), row(ln_a_w), row(ln_a_b))
        yg, cs_p, cs_s, st_p, st_s = _ssd(
            proj_b, ap, dp, l, n_pc, dec_batch, hist_s, st0, conv_s_w,
            row(conv_s_b), row(dskip), row(gnorm_w))
        m = _merge(c, yg, gates, l, row(b_gate), wa_bf, wb_bf)
        x, h = _out_proj(m, wo_bf, l, x,
                         norm_w[l + 1].reshape(1, -1) if l + 1 < depth else None)

        conv_a_p.append(ca_p)
        conv_a_s.append(ca_s)
        conv_s_p.append(cs_p)
        conv_s_s.append(cs_s)
        ssm_p.append(st_p.reshape(batch, N_HEADS, HEAD_DIM, D_STATE))
        ssm_s.append(st_s.reshape(dec_batch, N_HEADS, HEAD_DIM, D_STATE))

    y_p, y_s = _final_norm(x, final_norm_w.reshape(1, D_MODEL), n_p)
    y_prompt = y_p.reshape(batch, seq, D_MODEL)
    y_sample = y_s.reshape(dec_batch, dec_seq, D_MODEL)
    return (y_prompt, y_sample, jnp.stack(conv_a_p), jnp.stack(conv_s_p), jnp.stack(ssm_p),
            jnp.stack(conv_a_s), jnp.stack(conv_s_s), jnp.stack(ssm_s))
```

```python
import functools
import math

import jax
import jax.numpy as jnp
from jax import lax
from jax.experimental import pallas as pl
from jax.experimental.pallas import tpu as pltpu

F32 = jnp.float32
BF16 = jnp.bfloat16

D_MODEL = 2048
C_A = 2048
D_INNER = 4096
N_HEADS = 64
HEAD_DIM = 64
N_GROUPS = 8
D_STATE = 128
CONV_DIM = D_INNER + 2 * N_GROUPS * D_STATE
CONV_A_WIDTH = 31
SSM_CONV_WIDTH = 4
CHUNK = 64
NORM_EPS = 1e-6

COL_AVAL = 0
COL_AGLU = 2048
COL_AZ = 4096
COL_SZ = 0
COL_XS = 4096
COL_BC = 8192
W_COL_SZ = 3 * C_A
W_COL_DT = W_COL_SZ + D_INNER + CONV_DIM
W_COL_GATE = W_COL_DT + N_HEADS

A_HIST_ROWS = 32
CONV_A_STRIP = 256
S_HIST_ROWS = 8
PACK = 256
GROUP_LANES = D_INNER // N_GROUPS
VMEM_LIMIT = 56 * 1024 * 1024


def _cparams(sem):
    return pltpu.CompilerParams(dimension_semantics=sem, vmem_limit_bytes=VMEM_LIMIT)


def _rmsnorm_kernel(x_ref, w_ref, o_ref):
    x = x_ref[...]
    ms = jnp.mean(x * x, axis=-1, keepdims=True)
    o_ref[...] = (x * lax.rsqrt(ms + NORM_EPS) * w_ref[...]).astype(o_ref.dtype)


def _row_spec(width):
    return pl.BlockSpec((1, width), lambda *_: (0, 0))


def _rmsnorm(x, w_row, out_dtype, tm=512):
    n, d = x.shape
    return pl.pallas_call(
        _rmsnorm_kernel,
        grid=(n // tm,),
        in_specs=[pl.BlockSpec((tm, d), lambda i: (i, 0)), _row_spec(d)],
        out_specs=pl.BlockSpec((tm, d), lambda i: (i, 0)),
        out_shape=jax.ShapeDtypeStruct((n, d), out_dtype),
        compiler_params=_cparams(("parallel",)),
        name="rmsnorm",
    )(x, w_row)


def _final_norm_kernel(x_ref, w_ref, yp_ref, ys_ref, *, n_pt):
    i = pl.program_id(0)
    x = x_ref[...]
    ms = jnp.mean(x * x, axis=-1, keepdims=True)
    y = x * lax.rsqrt(ms + NORM_EPS) * w_ref[...]

    @pl.when(i < n_pt)
    def _():
        yp_ref[...] = y

    @pl.when(i >= n_pt)
    def _():
        ys_ref[...] = y


def _final_norm(x, w3, n_p, tm=512):
    n, d = x.shape
    tm = math.gcd(math.gcd(n_p, n - n_p), tm)
    n_pt = n_p // tm
    return pl.pallas_call(
        functools.partial(_final_norm_kernel, n_pt=n_pt),
        grid=(n // tm,),
        in_specs=[pl.BlockSpec((tm, d), lambda i: (i, 0)), _row_spec(d)],
        out_specs=[pl.BlockSpec((tm, d), lambda i: (jnp.minimum(i, n_pt - 1), 0)),
                   pl.BlockSpec((tm, d), lambda i: (jnp.maximum(i - n_pt, 0), 0))],
        out_shape=[jax.ShapeDtypeStruct((n_p, d), F32),
                   jax.ShapeDtypeStruct((n - n_p, d), F32)],
        compiler_params=_cparams(("arbitrary",)),
        name="final_norm",
    )(x, w3)


def _in_proj_kernel(h_ref, w_ref, o_ref, wbf):
    @pl.when(pl.program_id(1) == 0)
    def _():
        w = w_ref[0] if len(w_ref.shape) == 3 else w_ref[...]
        wbf[...] = w.astype(BF16)

    o_ref[...] = lax.dot_general(h_ref[...], wbf[...], (((1,), (1,)), ((), ())),
                                 preferred_element_type=F32)


def _in_proj(h, wt, l, col0, n_cols, tm=1536, tn=1024):
    n, k = h.shape
    if n % tm:
        tm = 1024
    n_w = n_cols // tn
    if col0 % tn == 0:
        t0 = col0 // tn
        w_spec = pl.BlockSpec((None, tn, k), lambda j, i: (l, t0 + j, 0))
    else:
        w_spec = pl.BlockSpec((pl.Element(1), pl.Element(tn), pl.Element(k)),
                              lambda j, i: (l, pl.multiple_of(col0 + j * tn, 8), 0))
    return pl.pallas_call(
        _in_proj_kernel,
        grid=(n_w, n // tm),
        in_specs=[pl.BlockSpec((tm, k), lambda j, i: (i, 0)), w_spec],
        out_specs=pl.BlockSpec((tm, tn), lambda j, i: (i, j)),
        out_shape=jax.ShapeDtypeStruct((n, n_cols), F32),
        scratch_shapes=[pltpu.VMEM((tn, k), BF16)],
        compiler_params=_cparams(("arbitrary", "arbitrary")),
        name="in_proj",
    )(h, wt)


def _conv_a_kernel(aval_ref, aglu_ref, az_ref, hist_ref, cw_ref, cb_ref, lnw_ref, lnb_ref,
                   c_ref, stp_ref, sts_ref, ubuf, cbuf, shbuf, *, n_pc):
    g = pl.program_id(0)
    t = CHUNK
    hr = A_HIST_ROWS
    nh = CONV_A_WIDTH - 1

    @pl.when(g == 0)
    def _():
        ubuf[...] = jnp.zeros(ubuf.shape, F32)

    ubuf[0:hr, :] = ubuf[t:t + hr, :]
    ubuf[hr - nh:hr, :] = jnp.where(g >= n_pc, hist_ref[0], ubuf[hr - nh:hr, :])
    ubuf[hr:hr + t, :] = aval_ref[...] * jax.nn.sigmoid(aglu_ref[...])

    base = hr - nh
    strip = CONV_A_STRIP
    sub = 8
    last_row = base + CONV_A_WIDTH - 1
    for s in range(C_A // strip):
        sl = slice(s * strip, (s + 1) * strip)
        n_q = [(last_row - r) // sub + 1 for r in range(sub)]
        for r in range(1, sub):
            rows = sub * (n_q[r] - 1) + t
            shbuf[r, 0:rows, :] = ubuf[r:r + rows, sl]
        acc = cb_ref[:, sl] + jnp.zeros((t, strip), F32)
        for r in range(sub):
            for q in range(n_q[r]):
                k = sub * q + r - base
                if k >= 0:
                    rows = slice(sub * q, sub * q + t)
                    x = ubuf[rows, sl] if r == 0 else shbuf[r, rows, :]
                    acc = acc + x * cw_ref[k:k + 1, sl]
        cbuf[:, sl] = acc

    c = cbuf[...]
    mu = jnp.mean(c, axis=-1, keepdims=True)
    cc = c - mu
    var = jnp.mean(cc * cc, axis=-1, keepdims=True)
    y = cc * lax.rsqrt(var + NORM_EPS) * lnw_ref[...] + lnb_ref[...]
    z = az_ref[...]
    out = (y * jax.nn.sigmoid(y)) * (z * jax.nn.sigmoid(z))
    c_ref[...] = out.astype(c_ref.dtype)

    sts_ref[0] = ubuf[hr + t - nh:hr + t, :]

    @pl.when(g == n_pc - 1)
    def _():
        stp_ref[0] = ubuf[hr + t - nh:hr + t, :]


def _conv_a(proj_a, l, n_pc, n_sc, hist, cw, cb, lnw, lnb):
    t = CHUNK
    nh = CONV_A_WIDTH - 1
    n_chunks = n_pc + n_sc
    w2 = C_A
    stream_of = lambda i: jnp.maximum(i - n_pc, 0)
    kern = functools.partial(_conv_a_kernel, n_pc=n_pc)
    return pl.pallas_call(
        kern,
        grid=(n_chunks,),
        in_specs=[pl.BlockSpec((t, w2), lambda i: (i, COL_AVAL // w2)),
                  pl.BlockSpec((t, w2), lambda i: (i, COL_AGLU // w2)),
                  pl.BlockSpec((t, w2), lambda i: (i, COL_AZ // w2)),
                  pl.BlockSpec((1, nh, C_A), lambda i: (l * n_sc + stream_of(i), 0, 0)),
                  pl.BlockSpec((None, CONV_A_WIDTH, C_A), lambda i: (l, 0, 0)),
                  _row_spec(C_A),
                  _row_spec(C_A),
                  _row_spec(C_A)],
        out_specs=[pl.BlockSpec((t, C_A), lambda i: (i, 0)),
                   pl.BlockSpec((1, nh, C_A), lambda i: (0, 0, 0)),
                   pl.BlockSpec((1, nh, C_A), lambda i: (stream_of(i), 0, 0))],
        out_shape=[jax.ShapeDtypeStruct((n_chunks * t, C_A), BF16),
                   jax.ShapeDtypeStruct((1, nh, C_A), F32),
                   jax.ShapeDtypeStruct((n_sc, nh, C_A), F32)],
        scratch_shapes=[pltpu.VMEM((A_HIST_ROWS + t, C_A), F32),
                        pltpu.VMEM((t, C_A), F32),
                        pltpu.VMEM((8, A_HIST_ROWS + t, CONV_A_STRIP), F32)],
        compiler_params=_cparams(("arbitrary",)),
        name="conv_a",
    )(proj_a, proj_a, proj_a, hist, cw, cb, lnw, lnb)


def _split_bf16(x, parts):
    out = []
    r = x
    for p in range(parts):
        hi = r.astype(BF16)
        out.append(hi)
        if p + 1 < parts:
            r = r - hi.astype(F32)
    return out


def _softplus(x):
    return jnp.maximum(x, 0.0) + jnp.log1p(jnp.exp(-jnp.abs(x)))


def _dt_chain_kernel(h_ref, wdt_ref, dtb_ref, alog_ref, ap_ref, dp_ref, tri_s):
    tm = h_ref.shape[0]
    dt_raw3 = lax.dot_general(h_ref[...], wdt_ref[...], (((1,), (1,)), ((), ())),
                              preferred_element_type=F32)
    dt3 = _softplus(dt_raw3 + dtb_ref[...])
    a3 = dt3 * (-jnp.exp(alog_ref[...]))
    @pl.when(pl.program_id(0) == 0)
    def _():
        ri = lax.broadcasted_iota(jnp.int32, (tm, tm), 0)
        ci = lax.broadcasted_iota(jnp.int32, (tm, tm), 1)
        chunk_bits = CHUNK.bit_length() - 1
        assert 1 << chunk_bits == CHUNK
        same_chunk = (ri >> chunk_bits) == (ci >> chunk_bits)
        tri_s[...] = jnp.where(same_chunk & (ci <= ri), 1.0, 0.0).astype(BF16)

    tri = tri_s[...]
    acum3 = jnp.zeros((tm, 3 * N_HEADS), F32)
    for part in _split_bf16(a3, 3):
        acum3 = acum3 + jnp.dot(tri, part, preferred_element_type=F32)
    lane3 = lax.broadcasted_iota(jnp.int32, (tm, 3 * N_HEADS), 1)
    a_hi, a_mid, a_lo = [p.astype(F32) for p in _split_bf16(acum3, 3)]
    ap_ref[...] = jnp.where(lane3 < N_HEADS, a_hi,
                            jnp.where(lane3 < 2 * N_HEADS, a_mid, a_lo)).astype(BF16)
    lane2 = lax.broadcasted_iota(jnp.int32, (tm, 2 * N_HEADS), 1)
    d_hi, d_lo = [p.astype(F32) for p in _split_bf16(dt3[:, 0:2 * N_HEADS], 2)]
    dp_ref[...] = jnp.where(lane2 < N_HEADS, d_hi, d_lo).astype(BF16)


def _dt_chain(h, wdt, l, dtb, alog, tm=512):
    n, k = h.shape
    return pl.pallas_call(
        _dt_chain_kernel,
        grid=(n // tm,),
        in_specs=[pl.BlockSpec((tm, k), lambda i: (i, 0)),
                  pl.BlockSpec((None, 3 * N_HEADS, k), lambda i: (l, 0, 0)),
                  _row_spec(3 * N_HEADS),
                  _row_spec(3 * N_HEADS)],
        out_specs=[pl.BlockSpec((tm, 3 * N_HEADS), lambda i: (i, 0)),
                   pl.BlockSpec((tm, 2 * N_HEADS), lambda i: (i, 0))],
        out_shape=[jax.ShapeDtypeStruct((n, 3 * N_HEADS), BF16),
                   jax.ShapeDtypeStruct((n, 2 * N_HEADS), BF16)],
        scratch_shapes=[pltpu.VMEM((tm, tm), BF16)],
        compiler_params=_cparams(("arbitrary",)),
        name="dt_chain",
    )(h, wdt, dtb, alog)


def _ssd_kernel(sz_ref, xs_ref, bc_ref, ap_s, dp_s, hist_ref, st0_ref, cw_ref, cb_ref,
                dskip_ref, gw_ref,
                yg_ref, cstp_ref, csts_ref, stp_ref, sts_ref,
                xbuf, xs_s, bc_s, st, e_s, ygbuf, *, n_pc):
    i = pl.program_id(0)
    t = CHUNK
    hr = S_HIST_ROWS
    nh = SSM_CONV_WIDTH - 1
    tb = 128

    @pl.when(i == 0)
    def _():
        hh = lax.broadcasted_iota(jnp.int32, (3 * N_HEADS, D_INNER), 0) & (N_HEADS - 1)
        jj = lax.broadcasted_iota(jnp.int32, (3 * N_HEADS, D_INNER), 1)
        e_s[...] = jnp.where((jj >> 6) == hh, 1.0, 0.0).astype(BF16)
        xbuf[0:hr, :] = jnp.zeros((hr, CONV_DIM), F32)
        st[...] = jnp.zeros((D_STATE, D_INNER), F32)

    @pl.when((i > 0) & (i < n_pc))
    def _():
        xbuf[0:hr, :] = xbuf[t:t + hr, :]

    @pl.when(i >= n_pc)
    def _():
        xbuf[hr - nh:hr, :] = hist_ref[0]
        for b in range(D_INNER // tb):
            st[:, b * tb:(b + 1) * tb] = st0_ref[0, b * tb:(b + 1) * tb, :].T

    xbuf[hr:hr + t, 0:D_INNER] = xs_ref[...]
    xbuf[hr:hr + t, D_INNER:CONV_DIM] = bc_ref[...]

    base = hr - nh
    strip = 512

    def conv_strip(s):
        sl = slice(s * strip, (s + 1) * strip)
        acc = jnp.zeros((t, strip), F32)
        for k in range(SSM_CONV_WIDTH):
            acc = acc + xbuf[base + k:base + k + t, sl] * cw_ref[k:k + 1, sl]
        acc = acc + cb_ref[:, sl]
        acc = acc * jax.nn.sigmoid(acc)
        if s * strip < D_INNER:
            xs_s[:, sl] = acc
        else:
            bc_s[:, s * strip - D_INNER:(s + 1) * strip - D_INNER] = acc

    for s in range(CONV_DIM // strip):
        conv_strip(s)

    li = lax.broadcasted_iota(jnp.int32, (t, PACK), 0)
    ji = lax.broadcasted_iota(jnp.int32, (t, PACK), 1)
    si = ji & (HEAD_DIM - 1)
    hi4 = ji >> 6
    diag_mask = li == si
    causal_mask = li >= si

    def group_body(g):
        goff = g * GROUP_LANES
        boff = g * D_STATE
        bm = bc_s[:, pl.ds(boff, D_STATE)].astype(BF16)
        cm = bc_s[:, pl.ds(N_GROUPS * D_STATE + boff, D_STATE)].astype(BF16)
        bm4 = jnp.concatenate([bm, bm, bm, bm], axis=0)
        cb_rep = lax.dot_general(cm, bm4, (((1,), (1,)), ((), ())),
                                 preferred_element_type=F32)
        st_g = st[:, pl.ds(goff, GROUP_LANES)]
        yoff = jnp.dot(cm, st_g.astype(BF16), preferred_element_type=F32)
        ss = jnp.zeros((t, 1), F32)
        for half in range(GROUP_LANES // PACK):
            lanes = pl.ds(goff + half * PACK, PACK)
            acol = jnp.dot(ap_s[...], e_s[:, lanes], preferred_element_type=F32)
            dtx = jnp.dot(dp_s[...], e_s[0:2 * N_HEADS, lanes], preferred_element_type=F32)
            xs = xs_s[:, lanes]
            xd = xs * dtx
            arow = jnp.sum(jnp.where(diag_mask, acol, 0.0), axis=0, keepdims=True)
            lmat = jnp.exp(jnp.where(causal_mask, acol - arow, -1e30))
            gmat = (cb_rep * lmat).astype(BF16)
            bd = jnp.concatenate(
                [jnp.where(hi4 == rr, xd, 0.0).astype(BF16) for rr in range(PACK // HEAD_DIM)],
                axis=0)
            ydiag = jnp.dot(gmat, bd, preferred_element_type=F32)
            y = (ydiag + jnp.exp(acol) * yoff[:, half * PACK:(half + 1) * PACK]
                 + dskip_ref[:, lanes] * xs)
            z = sz_ref[:, lanes]
            yg = y * (z * jax.nn.sigmoid(z))
            ygbuf[:, lanes] = yg
            ss = ss + jnp.sum(yg * yg, axis=-1, keepdims=True)
            alast = acol[t - 1:t, :]
            xdd = (xd * jnp.exp(alast - acol)).astype(BF16)
            upd = lax.dot_general(bm, xdd, (((0,), (0,)), ((), ())),
                                  preferred_element_type=F32)
            st[:, lanes] = st[:, lanes] * jnp.exp(alast) + upd
        scale = lax.rsqrt(ss * (1.0 / GROUP_LANES) + NORM_EPS)
        glanes = pl.ds(goff, GROUP_LANES)
        yg_ref[:, glanes] = (ygbuf[:, glanes] * scale * gw_ref[:, glanes]).astype(yg_ref.dtype)

    for g in range(N_GROUPS):
        group_body(g)

    def write_state(cst_ref, stout_ref):
        cst_ref[0] = xbuf[hr + t - nh:hr + t, :]
        for b in range(D_INNER // tb):
            stout_ref[0, b * tb:(b + 1) * tb, :] = st[:, b * tb:(b + 1) * tb].T

    @pl.when(i == n_pc - 1)
    def _():
        write_state(cstp_ref, stp_ref)

    @pl.when(i >= n_pc)
    def _():
        write_state(csts_ref, sts_ref)


def _ssd(proj, ap, dp, l, n_pc, n_sc, hist, st0, cw, cb, dskip, gw):
    t = CHUNK
    nh = SSM_CONV_WIDTH - 1
    n_chunks = n_pc + n_sc
    seq_of = lambda i: (jnp.maximum(i - n_pc, 0), 0, 0)
    seq_of_l = lambda i: (l * n_sc + jnp.maximum(i - n_pc, 0), 0, 0)
    first = lambda i: (0, 0, 0)
    kern = functools.partial(_ssd_kernel, n_pc=n_pc)
    return pl.pallas_call(
        kern,
        grid=(n_chunks,),
        in_specs=[pl.BlockSpec((t, D_INNER), lambda i: (i, COL_SZ // D_INNER)),
                  pl.BlockSpec((t, D_INNER), lambda i: (i, COL_XS // D_INNER)),
                  pl.BlockSpec((t, 2048), lambda i: (i, COL_BC // 2048)),
                  pl.BlockSpec((t, 3 * N_HEADS), lambda i: (i, 0)),
                  pl.BlockSpec((t, 2 * N_HEADS), lambda i: (i, 0)),
                  pl.BlockSpec((1, nh, CONV_DIM), seq_of_l),
                  pl.BlockSpec((1, D_INNER, D_STATE), seq_of_l),
                  pl.BlockSpec((None, SSM_CONV_WIDTH, CONV_DIM), lambda i: (l, 0, 0)),
                  _row_spec(CONV_DIM),
                  _row_spec(D_INNER),
                  _row_spec(D_INNER)],
        out_specs=[pl.BlockSpec((t, D_INNER), lambda i: (i, 0)),
                   pl.BlockSpec((1, nh, CONV_DIM), first),
                   pl.BlockSpec((1, nh, CONV_DIM), seq_of),
                   pl.BlockSpec((1, D_INNER, D_STATE), first),
                   pl.BlockSpec((1, D_INNER, D_STATE), seq_of)],
        out_shape=[jax.ShapeDtypeStruct((n_chunks * t, D_INNER), BF16),
                   jax.ShapeDtypeStruct((1, nh, CONV_DIM), F32),
                   jax.ShapeDtypeStruct((n_sc, nh, CONV_DIM), F32),
                   jax.ShapeDtypeStruct((1, D_INNER, D_STATE), F32),
                   jax.ShapeDtypeStruct((n_sc, D_INNER, D_STATE), F32)],
        scratch_shapes=[pltpu.VMEM((S_HIST_ROWS + t, CONV_DIM), F32),
                        pltpu.VMEM((t, D_INNER), F32),
                        pltpu.VMEM((t, 2 * N_GROUPS * D_STATE), F32),
                        pltpu.VMEM((D_STATE, D_INNER), F32),
                        pltpu.VMEM((3 * N_HEADS, D_INNER), BF16),
                        pltpu.VMEM((t, D_INNER), F32)],
        compiler_params=_cparams(("arbitrary",)),
        name="ssd",
    )(proj, proj, proj, ap, dp, hist, st0, cw, cb, dskip, gw)


def _merge_kernel(c_ref, yg_ref, ga_ref, gb_ref, bga_ref, bgb_ref, wa_ref, wb_ref, m_ref):
    a_out = jnp.dot(c_ref[...], wa_ref[...], preferred_element_type=F32)
    b_out = jnp.dot(yg_ref[...], wb_ref[...], preferred_element_type=F32)
    ga = jax.nn.sigmoid(ga_ref[...] + bga_ref[...])
    gb = jax.nn.sigmoid(gb_ref[...] + bgb_ref[...])
    m_ref[...] = (ga * a_out + gb * b_out).astype(m_ref.dtype)


def _merge(c, yg, gates, l, bg3, wa, wb, tm=512, tn=1024):
    n = c.shape[0]
    nj = D_MODEL // tn
    return pl.pallas_call(
        _merge_kernel,
        grid=(nj, n // tm),
        in_specs=[pl.BlockSpec((tm, C_A), lambda j, i: (i, 0)),
                  pl.BlockSpec((tm, D_INNER), lambda j, i: (i, 0)),
                  pl.BlockSpec((tm, tn), lambda j, i: (i, j)),
                  pl.BlockSpec((tm, tn), lambda j, i: (i, nj + j)),
                  pl.BlockSpec((1, tn), lambda j, i: (0, j)),
                  pl.BlockSpec((1, tn), lambda j, i: (0, nj + j)),
                  pl.BlockSpec((None, C_A, tn), lambda j, i: (l, 0, j)),
                  pl.BlockSpec((None, D_INNER, tn), lambda j, i: (l, 0, j))],
        out_specs=pl.BlockSpec((tm, tn), lambda j, i: (i, j)),
        out_shape=jax.ShapeDtypeStruct((n, D_MODEL), BF16),
        compiler_params=_cparams(("parallel", "parallel")),
        name="merge",
    )(c, yg, gates, gates, bg3, bg3, wa, wb)


def _out_kernel(m_ref, w_ref, x_ref, o_ref):
    o_ref[...] = x_ref[...] + jnp.dot(m_ref[...], w_ref[...], preferred_element_type=F32)


def _out_norm_kernel(m_ref, w_ref, x_ref, nw_ref, o_ref, h_ref):
    x = x_ref[...] + jnp.dot(m_ref[...], w_ref[...], preferred_element_type=F32)
    o_ref[...] = x
    ms = jnp.mean(x * x, axis=-1, keepdims=True)
    h_ref[...] = (x * lax.rsqrt(ms + NORM_EPS) * nw_ref[...]).astype(h_ref.dtype)


def _out_proj(m, w, l, x, next_norm_w=None, tm=512):
    n = m.shape[0]
    tile = pl.BlockSpec((tm, D_MODEL), lambda i: (i, 0))
    w_spec = pl.BlockSpec((None, D_MODEL, D_MODEL), lambda i: (l, 0, 0))
    x_shape = jax.ShapeDtypeStruct((n, D_MODEL), F32)
    if next_norm_w is None:
        return pl.pallas_call(
            _out_kernel, grid=(n // tm,), in_specs=[tile, w_spec, tile], out_specs=tile,
            out_shape=x_shape, compiler_params=_cparams(("parallel",)), name="out_proj",
        )(m, w, x), None
    return pl.pallas_call(
        _out_norm_kernel, grid=(n // tm,),
        in_specs=[tile, w_spec, tile, _row_spec(D_MODEL)], out_specs=[tile, tile],
        out_shape=[x_shape, jax.ShapeDtypeStruct((n, D_MODEL), BF16)],
        compiler_params=_cparams(("parallel",)), name="out_proj_norm",
    )(m, w, x, next_norm_w)


def kernel(x_prompt, x_sample, state_conv_a, state_conv_s, state_ssm, norm_w, w_in, b_gate,
           conv_a_w, conv_a_b, ln_a_w, ln_a_b, w_a_out, conv_s_w, conv_s_b, dt_bias, a_log,
           d_skip, gnorm_w, w_b_out, w_out, final_norm_w):
    depth = w_in.shape[0]
    batch, seq, _ = x_prompt.shape
    dec_batch, dec_seq, _ = x_sample.shape
    assert batch == 1 and dec_seq == CHUNK and seq % CHUNK == 0
    assert w_in.shape[2] == W_COL_GATE + 2 * D_MODEL
    n_p = batch * seq
    n_s = dec_batch * dec_seq
    n_pc = n_p // CHUNK

    x = jnp.concatenate([x_prompt.reshape(n_p, D_MODEL), x_sample.reshape(n_s, D_MODEL)], axis=0)
    w_in_t = jnp.swapaxes(w_in, 1, 2)
    w_dt_t = jnp.tile(w_in_t[:, W_COL_DT:W_COL_GATE, :], (1, 3, 1)).astype(BF16)
    dtb3 = jnp.tile(dt_bias, (1, 3))
    alog3 = jnp.tile(a_log, (1, 3))
    dskip = jnp.repeat(d_skip, HEAD_DIM, axis=1)
    st0 = state_ssm.reshape(depth * dec_batch, D_INNER, D_STATE)
    hist_a = state_conv_a.reshape(depth * dec_batch, CONV_A_WIDTH - 1, C_A)
    hist_s = state_conv_s.reshape(depth * dec_batch, SSM_CONV_WIDTH - 1, CONV_DIM)

    wa_bf, wb_bf, wo_bf = w_a_out.astype(BF16), w_b_out.astype(BF16), w_out.astype(BF16)

    conv_a_p, conv_s_p, ssm_p, conv_a_s, conv_s_s, ssm_s = [], [], [], [], [], []
    h = _rmsnorm(x, norm_w[0].reshape(1, -1), BF16)
    for l in range(depth):
        row = lambda v: v[l].reshape(1, -1)
        proj_a = _in_proj(h, w_in_t, l, 0, W_COL_SZ)
        proj_b = _in_proj(h, w_in_t, l, W_COL_SZ, W_COL_DT - W_COL_SZ)
        gates = _in_proj(h, w_in_t, l, W_COL_GATE, 2 * D_MODEL)
        ap, dp = _dt_chain(h, w_dt_t, l, row(dtb3), row(alog3))
        c, ca_p, ca_s = _conv_a(proj_a, l, n_pc, dec_batch, hist_a, conv_a_w,
                                row(conv_a_b), row(ln_a_w), row(ln_a_b))
        yg, cs_p, cs_s, st_p, st_s = _ssd(
            proj_b, ap, dp, l, n_pc, dec_batch, hist_s, st0, conv_s_w,
            row(conv_s_b), row(dskip), row(gnorm_w))
        m = _merge(c, yg, gates, l, row(b_gate), wa_bf, wb_bf)
        x, h = _out_proj(m, wo_bf, l, x,
                         norm_w[l + 1].reshape(1, -1) if l + 1 < depth else None)

        conv_a_p.append(ca_p)
        conv_a_s.append(ca_s)
        conv_s_p.append(cs_p)
        conv_s_s.append(cs_s)
        ssm_p.append(st_p.reshape(batch, N_HEADS, HEAD_DIM, D_STATE))
        ssm_s.append(st_s.reshape(dec_batch, N_HEADS, HEAD_DIM, D_STATE))

    y_p, y_s = _final_norm(x, final_norm_w.reshape(1, D_MODEL), n_p)
    y_prompt = y_p.reshape(batch, seq, D_MODEL)
    y_sample = y_s.reshape(dec_batch, dec_seq, D_MODEL)
    return (y_prompt, y_sample, jnp.stack(conv_a_p), jnp.stack(conv_s_p), jnp.stack(ssm_p),
            jnp.stack(conv_a_s), jnp.stack(conv_s_s), jnp.stack(ssm_s))
```

```python
import functools
import math

import jax
import jax.numpy as jnp
from jax import lax
from jax.experimental import pallas as pl
from jax.experimental.pallas import tpu as pltpu

F32 = jnp.float32
BF16 = jnp.bfloat16

D_MODEL = 2048
C_A = 2048
D_INNER = 4096
N_HEADS = 64
HEAD_DIM = 64
N_GROUPS = 8
D_STATE = 128
CONV_DIM = D_INNER + 2 * N_GROUPS * D_STATE
CONV_A_WIDTH = 31
SSM_CONV_WIDTH = 4
CHUNK = 64
NORM_EPS = 1e-6

COL_AVAL = 0
COL_AGLU = 2048
COL_AZ = 4096
COL_SZ = 0
COL_XS = 4096
COL_BC = 8192
W_COL_SZ = 3 * C_A
W_COL_DT = W_COL_SZ + D_INNER + CONV_DIM
W_COL_GATE = W_COL_DT + N_HEADS

A_HIST_ROWS = 32
CONV_A_STRIP = 256
S_HIST_ROWS = 8
PACK = 256
GROUP_LANES = D_INNER // N_GROUPS
VMEM_LIMIT = 56 * 1024 * 1024


def _cparams(sem):
    return pltpu.CompilerParams(dimension_semantics=sem, vmem_limit_bytes=VMEM_LIMIT)


def _row_spec(width):
    return pl.BlockSpec((1, width), lambda *_: (0, 0))


def _first_norm_kernel(xp_ref, xs_ref, w_ref, x_ref, h_ref, *, n_pt):
    i = pl.program_id(0)

    def emit(x):
        x_ref[...] = x
        ms = jnp.mean(x * x, axis=-1, keepdims=True)
        h_ref[...] = (x * lax.rsqrt(ms + NORM_EPS) * w_ref[...]).astype(h_ref.dtype)

    @pl.when(i < n_pt)
    def _():
        emit(xp_ref[...])

    @pl.when(i >= n_pt)
    def _():
        emit(xs_ref[...])


def _first_norm(x_p, x_s, w_row, tm=512):
    n_p, d = x_p.shape
    n_s = x_s.shape[0]
    tm = math.gcd(math.gcd(n_p, n_s), tm)
    n_pt = n_p // tm
    tile = pl.BlockSpec((tm, d), lambda i: (i, 0))
    return pl.pallas_call(
        functools.partial(_first_norm_kernel, n_pt=n_pt),
        grid=((n_p + n_s) // tm,),
        in_specs=[pl.BlockSpec((tm, d), lambda i: (jnp.minimum(i, n_pt - 1), 0)),
                  pl.BlockSpec((tm, d), lambda i: (jnp.maximum(i - n_pt, 0), 0)),
                  _row_spec(d)],
        out_specs=[tile, tile],
        out_shape=[jax.ShapeDtypeStruct((n_p + n_s, d), F32),
                   jax.ShapeDtypeStruct((n_p + n_s, d), BF16)],
        compiler_params=_cparams(("arbitrary",)),
        name="first_norm",
    )(x_p, x_s, w_row)


def _final_norm_kernel(x_ref, w_ref, yp_ref, ys_ref, *, n_pt):
    i = pl.program_id(0)
    x = x_ref[...]
    ms = jnp.mean(x * x, axis=-1, keepdims=True)
    y = x * lax.rsqrt(ms + NORM_EPS) * w_ref[...]

    @pl.when(i < n_pt)
    def _():
        yp_ref[...] = y

    @pl.when(i >= n_pt)
    def _():
        ys_ref[...] = y


def _final_norm(x, w3, n_p, tm=512):
    n, d = x.shape
    tm = math.gcd(math.gcd(n_p, n - n_p), tm)
    n_pt = n_p // tm
    return pl.pallas_call(
        functools.partial(_final_norm_kernel, n_pt=n_pt),
        grid=(n // tm,),
        in_specs=[pl.BlockSpec((tm, d), lambda i: (i, 0)), _row_spec(d)],
        out_specs=[pl.BlockSpec((tm, d), lambda i: (jnp.minimum(i, n_pt - 1), 0)),
                   pl.BlockSpec((tm, d), lambda i: (jnp.maximum(i - n_pt, 0), 0))],
        out_shape=[jax.ShapeDtypeStruct((n_p, d), F32),
                   jax.ShapeDtypeStruct((n - n_p, d), F32)],
        compiler_params=_cparams(("arbitrary",)),
        name="final_norm",
    )(x, w3)


def _in_proj_kernel(h_ref, w_ref, o_ref, wbf):
    @pl.when(pl.program_id(1) == 0)
    def _():
        w = w_ref[0] if len(w_ref.shape) == 3 else w_ref[...]
        wbf[...] = w.astype(BF16)

    o_ref[...] = lax.dot_general(h_ref[...], wbf[...], (((1,), (1,)), ((), ())),
                                 preferred_element_type=F32)


def _in_proj(h, wt, l, col0, n_cols, tm=1536, tn=1024):
    n, k = h.shape
    if n % tm:
        tm = 1024
    n_w = n_cols // tn
    if col0 % tn == 0:
        t0 = col0 // tn
        w_spec = pl.BlockSpec((None, tn, k), lambda j, i: (l, t0 + j, 0))
    else:
        w_spec = pl.BlockSpec((pl.Element(1), pl.Element(tn), pl.Element(k)),
                              lambda j, i: (l, pl.multiple_of(col0 + j * tn, 8), 0))
    return pl.pallas_call(
        _in_proj_kernel,
        grid=(n_w, n // tm),
        in_specs=[pl.BlockSpec((tm, k), lambda j, i: (i, 0)), w_spec],
        out_specs=pl.BlockSpec((tm, tn), lambda j, i: (i, j)),
        out_shape=jax.ShapeDtypeStruct((n, n_cols), F32),
        scratch_shapes=[pltpu.VMEM((tn, k), BF16)],
        compiler_params=_cparams(("arbitrary", "arbitrary")),
        name="in_proj",
    )(h, wt)


def _conv_a_kernel(aval_ref, aglu_ref, az_ref, hist_ref, cw_ref, cb_ref, lnw_ref, lnb_ref,
                   c_ref, stp_ref, sts_ref, ubuf, cbuf, shbuf, *, n_pc):
    g = pl.program_id(0)
    t = CHUNK
    hr = A_HIST_ROWS
    nh = CONV_A_WIDTH - 1

    @pl.when(g == 0)
    def _():
        ubuf[...] = jnp.zeros(ubuf.shape, F32)

    ubuf[0:hr, :] = ubuf[t:t + hr, :]
    ubuf[hr - nh:hr, :] = jnp.where(g >= n_pc, hist_ref[0], ubuf[hr - nh:hr, :])
    ubuf[hr:hr + t, :] = aval_ref[...] * jax.nn.sigmoid(aglu_ref[...])

    base = hr - nh
    strip = CONV_A_STRIP
    sub = 8
    last_row = base + CONV_A_WIDTH - 1
    for s in range(C_A // strip):
        sl = slice(s * strip, (s + 1) * strip)
        n_q = [(last_row - r) // sub + 1 for r in range(sub)]
        for r in range(1, sub):
            rows = sub * (n_q[r] - 1) + t
            shbuf[r, 0:rows, :] = ubuf[r:r + rows, sl]
        acc = cb_ref[:, sl] + jnp.zeros((t, strip), F32)
        for r in range(sub):
            for q in range(n_q[r]):
                k = sub * q + r - base
                if k >= 0:
                    rows = slice(sub * q, sub * q + t)
                    x = ubuf[rows, sl] if r == 0 else shbuf[r, rows, :]
                    acc = acc + x * cw_ref[k:k + 1, sl]
        cbuf[:, sl] = acc

    c = cbuf[...]
    mu = jnp.mean(c, axis=-1, keepdims=True)
    cc = c - mu
    var = jnp.mean(cc * cc, axis=-1, keepdims=True)
    y = cc * lax.rsqrt(var + NORM_EPS) * lnw_ref[...] + lnb_ref[...]
    z = az_ref[...]
    out = (y * jax.nn.sigmoid(y)) * (z * jax.nn.sigmoid(z))
    c_ref[...] = out.astype(c_ref.dtype)

    sts_ref[0] = ubuf[hr + t - nh:hr + t, :]

    @pl.when(g == n_pc - 1)
    def _():
        stp_ref[0] = ubuf[hr + t - nh:hr + t, :]


def _conv_a(proj_a, l, n_pc, n_sc, hist, cw, cb, lnw, lnb):
    t = CHUNK
    nh = CONV_A_WIDTH - 1
    n_chunks = n_pc + n_sc
    w2 = C_A
    stream_of = lambda i: jnp.maximum(i - n_pc, 0)
    kern = functools.partial(_conv_a_kernel, n_pc=n_pc)
    return pl.pallas_call(
        kern,
        grid=(n_chunks,),
        in_specs=[pl.BlockSpec((t, w2), lambda i: (i, COL_AVAL // w2)),
                  pl.BlockSpec((t, w2), lambda i: (i, COL_AGLU // w2)),
                  pl.BlockSpec((t, w2), lambda i: (i, COL_AZ // w2)),
                  pl.BlockSpec((1, nh, C_A), lambda i: (l * n_sc + stream_of(i), 0, 0)),
                  pl.BlockSpec((None, CONV_A_WIDTH, C_A), lambda i: (l, 0, 0)),
                  _row_spec(C_A),
                  _row_spec(C_A),
                  _row_spec(C_A)],
        out_specs=[pl.BlockSpec((t, C_A), lambda i: (i, 0)),
                   pl.BlockSpec((1, nh, C_A), lambda i: (0, 0, 0)),
                   pl.BlockSpec((1, nh, C_A), lambda i: (stream_of(i), 0, 0))],
        out_shape=[jax.ShapeDtypeStruct((n_chunks * t, C_A), BF16),
                   jax.ShapeDtypeStruct((1, nh, C_A), F32),
                   jax.ShapeDtypeStruct((n_sc, nh, C_A), F32)],
        scratch_shapes=[pltpu.VMEM((A_HIST_ROWS + t, C_A), F32),
                        pltpu.VMEM((t, C_A), F32),
                        pltpu.VMEM((8, A_HIST_ROWS + t, CONV_A_STRIP), F32)],
        compiler_params=_cparams(("arbitrary",)),
        name="conv_a",
    )(proj_a, proj_a, proj_a, hist, cw, cb, lnw, lnb)


def _split_bf16(x, parts):
    out = []
    r = x
    for p in range(parts):
        hi = r.astype(BF16)
        out.append(hi)
        if p + 1 < parts:
            r = r - hi.astype(F32)
    return out


def _softplus(x):
    return jnp.maximum(x, 0.0) + jnp.log1p(jnp.exp(-jnp.abs(x)))


def _dt_chain_kernel(h_ref, wdt_ref, dtb_ref, alog_ref, ap_ref, dp_ref, tri_s):
    tm = h_ref.shape[0]
    dt_raw3 = lax.dot_general(h_ref[...], wdt_ref[...], (((1,), (1,)), ((), ())),
                              preferred_element_type=F32)
    dt3 = _softplus(dt_raw3 + dtb_ref[...])
    a3 = dt3 * (-jnp.exp(alog_ref[...]))
    @pl.when(pl.program_id(0) == 0)
    def _():
        ri = lax.broadcasted_iota(jnp.int32, (tm, tm), 0)
        ci = lax.broadcasted_iota(jnp.int32, (tm, tm), 1)
        chunk_bits = CHUNK.bit_length() - 1
        assert 1 << chunk_bits == CHUNK
        same_chunk = (ri >> chunk_bits) == (ci >> chunk_bits)
        tri_s[...] = jnp.where(same_chunk & (ci <= ri), 1.0, 0.0).astype(BF16)

    tri = tri_s[...]
    acum3 = jnp.zeros((tm, 3 * N_HEADS), F32)
    for part in _split_bf16(a3, 3):
        acum3 = acum3 + jnp.dot(tri, part, preferred_element_type=F32)
    lane3 = lax.broadcasted_iota(jnp.int32, (tm, 3 * N_HEADS), 1)
    a_hi, a_mid, a_lo = [p.astype(F32) for p in _split_bf16(acum3, 3)]
    ap_ref[...] = jnp.where(lane3 < N_HEADS, a_hi,
                            jnp.where(lane3 < 2 * N_HEADS, a_mid, a_lo)).astype(BF16)
    lane2 = lax.broadcasted_iota(jnp.int32, (tm, 2 * N_HEADS), 1)
    d_hi, d_lo = [p.astype(F32) for p in _split_bf16(dt3[:, 0:2 * N_HEADS], 2)]
    dp_ref[...] = jnp.where(lane2 < N_HEADS, d_hi, d_lo).astype(BF16)


def _dt_chain(h, wdt, l, dtb, alog, tm=512):
    n, k = h.shape
    return pl.pallas_call(
        _dt_chain_kernel,
        grid=(n // tm,),
        in_specs=[pl.BlockSpec((tm, k), lambda i: (i, 0)),
                  pl.BlockSpec((None, 3 * N_HEADS, k), lambda i: (l, 0, 0)),
                  _row_spec(3 * N_HEADS),
                  _row_spec(3 * N_HEADS)],
        out_specs=[pl.BlockSpec((tm, 3 * N_HEADS), lambda i: (i, 0)),
                   pl.BlockSpec((tm, 2 * N_HEADS), lambda i: (i, 0))],
        out_shape=[jax.ShapeDtypeStruct((n, 3 * N_HEADS), BF16),
                   jax.ShapeDtypeStruct((n, 2 * N_HEADS), BF16)],
        scratch_shapes=[pltpu.VMEM((tm, tm), BF16)],
        compiler_params=_cparams(("arbitrary",)),
        name="dt_chain",
    )(h, wdt, dtb, alog)


def _ssd_kernel(sz_ref, xs_ref, bc_ref, ap_s, dp_s, hist_ref, st0_ref, cw_ref, cb_ref,
                dskip_ref, gw_ref,
                yg_ref, cstp_ref, csts_ref, stp_ref, sts_ref,
                xbuf, xs_s, bc_s, st, e_s, ygbuf, *, n_pc):
    i = pl.program_id(0)
    t = CHUNK
    hr = S_HIST_ROWS
    nh = SSM_CONV_WIDTH - 1
    tb = 128

    @pl.when(i == 0)
    def _():
        hh = lax.broadcasted_iota(jnp.int32, (3 * N_HEADS, D_INNER), 0) & (N_HEADS - 1)
        jj = lax.broadcasted_iota(jnp.int32, (3 * N_HEADS, D_INNER), 1)
        e_s[...] = jnp.where((jj >> 6) == hh, 1.0, 0.0).astype(BF16)
        xbuf[0:hr, :] = jnp.zeros((hr, CONV_DIM), F32)
        st[...] = jnp.zeros((D_STATE, D_INNER), F32)

    @pl.when((i > 0) & (i < n_pc))
    def _():
        xbuf[0:hr, :] = xbuf[t:t + hr, :]

    @pl.when(i >= n_pc)
    def _():
        xbuf[hr - nh:hr, :] = hist_ref[0]
        for b in range(D_INNER // tb):
            st[:, b * tb:(b + 1) * tb] = st0_ref[0, b * tb:(b + 1) * tb, :].T

    xbuf[hr:hr + t, 0:D_INNER] = xs_ref[...]
    xbuf[hr:hr + t, D_INNER:CONV_DIM] = bc_ref[...]

    strip = 512

    def conv_strip(s):
        sl = slice(s * strip, (s + 1) * strip)
        win = xbuf[:, sl]
        part = win * cw_ref[0:1, sl]
        for k in range(1, SSM_CONV_WIDTH):
            part = pltpu.roll(part, 1, axis=0) + win * cw_ref[k:k + 1, sl]
        acc = part[hr:hr + t] + cb_ref[:, sl]
        acc = acc * jax.nn.sigmoid(acc)
        if s * strip < D_INNER:
            xs_s[:, sl] = acc
        else:
            bc_s[:, s * strip - D_INNER:(s + 1) * strip - D_INNER] = acc

    for s in range(CONV_DIM // strip):
        conv_strip(s)

    li = lax.broadcasted_iota(jnp.int32, (t, PACK), 0)
    ji = lax.broadcasted_iota(jnp.int32, (t, PACK), 1)
    si = ji & (HEAD_DIM - 1)
    hi4 = ji >> 6
    diag_mask = li == si
    causal_mask = li >= si

    def group_body(g):
        goff = g * GROUP_LANES
        boff = g * D_STATE
        bm = bc_s[:, pl.ds(boff, D_STATE)].astype(BF16)
        cm = bc_s[:, pl.ds(N_GROUPS * D_STATE + boff, D_STATE)].astype(BF16)
        bm4 = jnp.concatenate([bm, bm, bm, bm], axis=0)
        cb_rep = lax.dot_general(cm, bm4, (((1,), (1,)), ((), ())),
                                 preferred_element_type=F32)
        ss = jnp.zeros((t, 1), F32)
        for half in range(GROUP_LANES // PACK):
            lanes = pl.ds(goff + half * PACK, PACK)
            yoff = jnp.dot(cm, st[:, lanes].astype(BF16), preferred_element_type=F32)
            acol = jnp.dot(ap_s[...], e_s[:, lanes], preferred_element_type=F32)
            dtx = jnp.dot(dp_s[...], e_s[0:2 * N_HEADS, lanes], preferred_element_type=F32)
            xs = xs_s[:, lanes]
            xd = xs * dtx
            arow = jnp.sum(jnp.where(diag_mask, acol, 0.0), axis=0, keepdims=True)
            lmat = jnp.exp(jnp.where(causal_mask, acol - arow, -1e30))
            gmat = (cb_rep * lmat).astype(BF16)
            bd = jnp.concatenate(
                [jnp.where(hi4 == rr, xd, 0.0).astype(BF16) for rr in range(PACK // HEAD_DIM)],
                axis=0)
            ydiag = jnp.dot(gmat, bd, preferred_element_type=F32)
            y = ydiag + jnp.exp(acol) * yoff + dskip_ref[:, lanes] * xs
            z = sz_ref[:, lanes]
            yg = y * (z * jax.nn.sigmoid(z))
            ygbuf[:, lanes] = yg
            ss = ss + jnp.sum(yg * yg, axis=-1, keepdims=True)
            alast = acol[t - 1:t, :]
            xdd = (xd * jnp.exp(alast - acol)).astype(BF16)
            upd = lax.dot_general(bm, xdd, (((0,), (0,)), ((), ())),
                                  preferred_element_type=F32)
            st[:, lanes] = st[:, lanes] * jnp.exp(alast) + upd
        scale = lax.rsqrt(ss * (1.0 / GROUP_LANES) + NORM_EPS)
        glanes = pl.ds(goff, GROUP_LANES)
        yg_ref[:, glanes] = (ygbuf[:, glanes] * scale * gw_ref[:, glanes]).astype(yg_ref.dtype)

    for g in range(N_GROUPS):
        group_body(g)

    def write_state(cst_ref, stout_ref):
        cst_ref[0] = xbuf[hr + t - nh:hr + t, :]
        for b in range(D_INNER // tb):
            stout_ref[0, b * tb:(b + 1) * tb, :] = st[:, b * tb:(b + 1) * tb].T

    @pl.when(i == n_pc - 1)
    def _():
        write_state(cstp_ref, stp_ref)

    @pl.when(i >= n_pc)
    def _():
        write_state(csts_ref, sts_ref)


def _ssd(proj, ap, dp, l, n_pc, n_sc, hist, st0, cw, cb, dskip, gw):
    t = CHUNK
    nh = SSM_CONV_WIDTH - 1
    n_chunks = n_pc + n_sc
    seq_of = lambda i: (jnp.maximum(i - n_pc, 0), 0, 0)
    seq_of_l = lambda i: (l * n_sc + jnp.maximum(i - n_pc, 0), 0, 0)
    first = lambda i: (0, 0, 0)
    kern = functools.partial(_ssd_kernel, n_pc=n_pc)
    return pl.pallas_call(
        kern,
        grid=(n_chunks,),
        in_specs=[pl.BlockSpec((t, D_INNER), lambda i: (i, COL_SZ // D_INNER)),
                  pl.BlockSpec((t, D_INNER), lambda i: (i, COL_XS // D_INNER)),
                  pl.BlockSpec((t, 2048), lambda i: (i, COL_BC // 2048)),
                  pl.BlockSpec((t, 3 * N_HEADS), lambda i: (i, 0)),
                  pl.BlockSpec((t, 2 * N_HEADS), lambda i: (i, 0)),
                  pl.BlockSpec((1, nh, CONV_DIM), seq_of_l),
                  pl.BlockSpec((1, D_INNER, D_STATE), seq_of_l),
                  pl.BlockSpec((None, SSM_CONV_WIDTH, CONV_DIM), lambda i: (l, 0, 0)),
                  _row_spec(CONV_DIM),
                  _row_spec(D_INNER),
                  _row_spec(D_INNER)],
        out_specs=[pl.BlockSpec((t, D_INNER), lambda i: (i, 0)),
                   pl.BlockSpec((1, nh, CONV_DIM), first),
                   pl.BlockSpec((1, nh, CONV_DIM), seq_of),
                   pl.BlockSpec((1, D_INNER, D_STATE), first),
                   pl.BlockSpec((1, D_INNER, D_STATE), seq_of)],
        out_shape=[jax.ShapeDtypeStruct((n_chunks * t, D_INNER), BF16),
                   jax.ShapeDtypeStruct((1, nh, CONV_DIM), F32),
                   jax.ShapeDtypeStruct((n_sc, nh, CONV_DIM), F32),
                   jax.ShapeDtypeStruct((1, D_INNER, D_STATE), F32),
                   jax.ShapeDtypeStruct((n_sc, D_INNER, D_STATE), F32)],
        scratch_shapes=[pltpu.VMEM((S_HIST_ROWS + t, CONV_DIM), F32),
                        pltpu.VMEM((t, D_INNER), F32),
                        pltpu.VMEM((t, 2 * N_GROUPS * D_STATE), F32),
                        pltpu.VMEM((D_STATE, D_INNER), F32),
                        pltpu.VMEM((3 * N_HEADS, D_INNER), BF16),
                        pltpu.VMEM((t, D_INNER), F32)],
        compiler_params=_cparams(("arbitrary",)),
        name="ssd",
    )(proj, proj, proj, ap, dp, hist, st0, cw, cb, dskip, gw)


def _merge_kernel(c_ref, yg_ref, ga_ref, gb_ref, bga_ref, bgb_ref, wa_ref, wb_ref, m_ref):
    a_out = jnp.dot(c_ref[...], wa_ref[...], preferred_element_type=F32)
    b_out = jnp.dot(yg_ref[...], wb_ref[...], preferred_element_type=F32)
    ga = jax.nn.sigmoid(ga_ref[...] + bga_ref[...])
    gb = jax.nn.sigmoid(gb_ref[...] + bgb_ref[...])
    m_ref[...] = (ga * a_out + gb * b_out).astype(m_ref.dtype)


def _merge(c, yg, gates, l, bg3, wa, wb, tm=512, tn=1024):
    n = c.shape[0]
    nj = D_MODEL // tn
    return pl.pallas_call(
        _merge_kernel,
        grid=(nj, n // tm),
        in_specs=[pl.BlockSpec((tm, C_A), lambda j, i: (i, 0)),
                  pl.BlockSpec((tm, D_INNER), lambda j, i: (i, 0)),
                  pl.BlockSpec((tm, tn), lambda j, i: (i, j)),
                  pl.BlockSpec((tm, tn), lambda j, i: (i, nj + j)),
                  pl.BlockSpec((1, tn), lambda j, i: (0, j)),
                  pl.BlockSpec((1, tn), lambda j, i: (0, nj + j)),
                  pl.BlockSpec((None, C_A, tn), lambda j, i: (l, 0, j)),
                  pl.BlockSpec((None, D_INNER, tn), lambda j, i: (l, 0, j))],
        out_specs=pl.BlockSpec((tm, tn), lambda j, i: (i, j)),
        out_shape=jax.ShapeDtypeStruct((n, D_MODEL), BF16),
        compiler_params=_cparams(("parallel", "parallel")),
        name="merge",
    )(c, yg, gates, gates, bg3, bg3, wa, wb)


def _out_kernel(m_ref, w_ref, x_ref, o_ref):
    o_ref[...] = x_ref[...] + jnp.dot(m_ref[...], w_ref[...], preferred_element_type=F32)


def _out_norm_kernel(m_ref, w_ref, x_ref, nw_ref, o_ref, h_ref):
    x = x_ref[...] + jnp.dot(m_ref[...], w_ref[...], preferred_element_type=F32)
    o_ref[...] = x
    ms = jnp.mean(x * x, axis=-1, keepdims=True)
    h_ref[...] = (x * lax.rsqrt(ms + NORM_EPS) * nw_ref[...]).astype(h_ref.dtype)


def _out_proj(m, w, l, x, next_norm_w=None, tm=512):
    n = m.shape[0]
    tile = pl.BlockSpec((tm, D_MODEL), lambda i: (i, 0))
    w_spec = pl.BlockSpec((None, D_MODEL, D_MODEL), lambda i: (l, 0, 0))
    x_shape = jax.ShapeDtypeStruct((n, D_MODEL), F32)
    if next_norm_w is None:
        return pl.pallas_call(
            _out_kernel, grid=(n // tm,), in_specs=[tile, w_spec, tile], out_specs=tile,
            out_shape=x_shape, compiler_params=_cparams(("parallel",)), name="out_proj",
        )(m, w, x), None
    return pl.pallas_call(
        _out_norm_kernel, grid=(n // tm,),
        in_specs=[tile, w_spec, tile, _row_spec(D_MODEL)], out_specs=[tile, tile],
        out_shape=[x_shape, jax.ShapeDtypeStruct((n, D_MODEL), BF16)],
        compiler_params=_cparams(("parallel",)), name="out_proj_norm",
    )(m, w, x, next_norm_w)


def kernel(x_prompt, x_sample, state_conv_a, state_conv_s, state_ssm, norm_w, w_in, b_gate,
           conv_a_w, conv_a_b, ln_a_w, ln_a_b, w_a_out, conv_s_w, conv_s_b, dt_bias, a_log,
           d_skip, gnorm_w, w_b_out, w_out, final_norm_w):
    depth = w_in.shape[0]
    batch, seq, _ = x_prompt.shape
    dec_batch, dec_seq, _ = x_sample.shape
    assert batch == 1 and dec_seq == CHUNK and seq % CHUNK == 0
    assert w_in.shape[2] == W_COL_GATE + 2 * D_MODEL
    n_p = batch * seq
    n_s = dec_batch * dec_seq
    n_pc = n_p // CHUNK

    w_in_t = jnp.swapaxes(w_in, 1, 2)
    w_dt_t = jnp.tile(w_in_t[:, W_COL_DT:W_COL_GATE, :], (1, 3, 1)).astype(BF16)
    dtb3 = jnp.tile(dt_bias, (1, 3))
    alog3 = jnp.tile(a_log, (1, 3))
    dskip = jnp.repeat(d_skip, HEAD_DIM, axis=1)
    st0 = state_ssm.reshape(depth * dec_batch, D_INNER, D_STATE)
    hist_a = state_conv_a.reshape(depth * dec_batch, CONV_A_WIDTH - 1, C_A)
    hist_s = state_conv_s.reshape(depth * dec_batch, SSM_CONV_WIDTH - 1, CONV_DIM)

    wa_bf, wb_bf, wo_bf = w_a_out.astype(BF16), w_b_out.astype(BF16), w_out.astype(BF16)

    conv_a_p, conv_s_p, ssm_p, conv_a_s, conv_s_s, ssm_s = [], [], [], [], [], []
    x, h = _first_norm(x_prompt.reshape(n_p, D_MODEL), x_sample.reshape(n_s, D_MODEL),
                       norm_w[0].reshape(1, -1))
    for l in range(depth):
        row = lambda v: v[l].reshape(1, -1)
        proj_a = _in_proj(h, w_in_t, l, 0, W_COL_SZ)
        proj_b = _in_proj(h, w_in_t, l, W_COL_SZ, W_COL_DT - W_COL_SZ)
        gates = _in_proj(h, w_in_t, l, W_COL_GATE, 2 * D_MODEL)
        ap, dp = _dt_chain(h, w_dt_t, l, row(dtb3), row(alog3))
        c, ca_p, ca_s = _conv_a(proj_a, l, n_pc, dec_batch, hist_a, conv_a_w,
                                row(conv_a_b), row(ln_a_w), row(ln_a_b))
        yg, cs_p, cs_s, st_p, st_s = _ssd(
            proj_b, ap, dp, l, n_pc, dec_batch, hist_s, st0, conv_s_w,
            row(conv_s_b), row(dskip), row(gnorm_w))
        m = _merge(c, yg, gates, l, row(b_gate), wa_bf, wb_bf)
        x, h = _out_proj(m, wo_bf, l, x,
                         norm_w[l + 1].reshape(1, -1) if l + 1 < depth else None)

        conv_a_p.append(ca_p)
        conv_a_s.append(ca_s)
        conv_s_p.append(cs_p)
        conv_s_s.append(cs_s)
        ssm_p.append(st_p.reshape(batch, N_HEADS, HEAD_DIM, D_STATE))
        ssm_s.append(st_s.reshape(dec_batch, N_HEADS, HEAD_DIM, D_STATE))

    y_p, y_s = _final_norm(x, final_norm_w.reshape(1, D_MODEL), n_p)
    y_prompt = y_p.reshape(batch, seq, D_MODEL)
    y_sample = y_s.reshape(dec_batch, dec_seq, D_MODEL)
    return (y_prompt, y_sample, jnp.stack(conv_a_p), jnp.stack(conv_s_p), jnp.stack(ssm_p),
            jnp.stack(conv_a_s), jnp.stack(conv_s_s), jnp.stack(ssm_s))
```

```python
import functools
import math

import jax
import jax.numpy as jnp
from jax import lax
from jax.experimental import pallas as pl
from jax.experimental.pallas import tpu as pltpu

F32 = jnp.float32
BF16 = jnp.bfloat16

D_MODEL = 2048
C_A = 2048
D_INNER = 4096
N_HEADS = 64
HEAD_DIM = 64
N_GROUPS = 8
D_STATE = 128
CONV_DIM = D_INNER + 2 * N_GROUPS * D_STATE
CONV_A_WIDTH = 31
SSM_CONV_WIDTH = 4
CHUNK = 64
NORM_EPS = 1e-6

COL_AVAL = 0
COL_AGLU = 2048
COL_AZ = 4096
COL_SZ = 0
COL_XS = 4096
COL_BC = 8192
W_COL_SZ = 3 * C_A
W_COL_DT = W_COL_SZ + D_INNER + CONV_DIM
W_COL_GATE = W_COL_DT + N_HEADS

A_HIST_ROWS = 32
CONV_A_STRIP = 256
S_HIST_ROWS = 8
PACK = 256
GROUP_LANES = D_INNER // N_GROUPS
VMEM_LIMIT = 56 * 1024 * 1024


def _cparams(sem):
    return pltpu.CompilerParams(dimension_semantics=sem, vmem_limit_bytes=VMEM_LIMIT)


def _row_spec(width):
    return pl.BlockSpec((1, width), lambda *_: (0, 0))


def _first_norm_kernel(xp_ref, xs_ref, w_ref, x_ref, h_ref, *, n_pt):
    i = pl.program_id(0)

    def emit(x):
        x_ref[...] = x
        ms = jnp.mean(x * x, axis=-1, keepdims=True)
        h_ref[...] = (x * lax.rsqrt(ms + NORM_EPS) * w_ref[...]).astype(h_ref.dtype)

    @pl.when(i < n_pt)
    def _():
        emit(xp_ref[...])

    @pl.when(i >= n_pt)
    def _():
        emit(xs_ref[...])


def _first_norm(x_p, x_s, w_row, tm=512):
    n_p, d = x_p.shape
    n_s = x_s.shape[0]
    tm = math.gcd(math.gcd(n_p, n_s), tm)
    n_pt = n_p // tm
    tile = pl.BlockSpec((tm, d), lambda i: (i, 0))
    return pl.pallas_call(
        functools.partial(_first_norm_kernel, n_pt=n_pt),
        grid=((n_p + n_s) // tm,),
        in_specs=[pl.BlockSpec((tm, d), lambda i: (jnp.minimum(i, n_pt - 1), 0)),
                  pl.BlockSpec((tm, d), lambda i: (jnp.maximum(i - n_pt, 0), 0)),
                  _row_spec(d)],
        out_specs=[tile, tile],
        out_shape=[jax.ShapeDtypeStruct((n_p + n_s, d), F32),
                   jax.ShapeDtypeStruct((n_p + n_s, d), BF16)],
        compiler_params=_cparams(("arbitrary",)),
        name="first_norm",
    )(x_p, x_s, w_row)


def _in_proj_kernel(h_ref, w_ref, o_ref, wbf):
    @pl.when(pl.program_id(1) == 0)
    def _():
        w = w_ref[0] if len(w_ref.shape) == 3 else w_ref[...]
        wbf[...] = w.astype(BF16)

    o_ref[...] = lax.dot_general(h_ref[...], wbf[...], (((1,), (1,)), ((), ())),
                                 preferred_element_type=F32)


def _in_proj(h, wt, l, col0, n_cols, tm=1536, tn=1024):
    n, k = h.shape
    if n % tm:
        tm = 1024
    n_w = n_cols // tn
    if col0 % tn == 0:
        t0 = col0 // tn
        w_spec = pl.BlockSpec((None, tn, k), lambda j, i: (l, t0 + j, 0))
    else:
        w_spec = pl.BlockSpec((pl.Element(1), pl.Element(tn), pl.Element(k)),
                              lambda j, i: (l, pl.multiple_of(col0 + j * tn, 8), 0))
    return pl.pallas_call(
        _in_proj_kernel,
        grid=(n_w, n // tm),
        in_specs=[pl.BlockSpec((tm, k), lambda j, i: (i, 0)), w_spec],
        out_specs=pl.BlockSpec((tm, tn), lambda j, i: (i, j)),
        out_shape=jax.ShapeDtypeStruct((n, n_cols), F32),
        scratch_shapes=[pltpu.VMEM((tn, k), BF16)],
        compiler_params=_cparams(("arbitrary", "arbitrary")),
        name="in_proj",
    )(h, wt)


def _conv_a_kernel(aval_ref, aglu_ref, az_ref, hist_ref, cw_ref, cb_ref, lnw_ref, lnb_ref,
                   c_ref, stp_ref, sts_ref, ubuf, cbuf, shbuf, *, n_pc):
    g = pl.program_id(0)
    t = CHUNK
    hr = A_HIST_ROWS
    nh = CONV_A_WIDTH - 1

    @pl.when(g == 0)
    def _():
        ubuf[...] = jnp.zeros(ubuf.shape, F32)

    ubuf[0:hr, :] = ubuf[t:t + hr, :]
    ubuf[hr - nh:hr, :] = jnp.where(g >= n_pc, hist_ref[0], ubuf[hr - nh:hr, :])
    ubuf[hr:hr + t, :] = aval_ref[...] * jax.nn.sigmoid(aglu_ref[...])

    base = hr - nh
    strip = CONV_A_STRIP
    sub = 8
    last_row = base + CONV_A_WIDTH - 1
    for s in range(C_A // strip):
        sl = slice(s * strip, (s + 1) * strip)
        n_q = [(last_row - r) // sub + 1 for r in range(sub)]
        for r in range(1, sub):
            rows = sub * (n_q[r] - 1) + t
            shbuf[r, 0:rows, :] = ubuf[r:r + rows, sl]
        acc = cb_ref[:, sl] + jnp.zeros((t, strip), F32)
        for r in range(sub):
            for q in range(n_q[r]):
                k = sub * q + r - base
                if k >= 0:
                    rows = slice(sub * q, sub * q + t)
                    x = ubuf[rows, sl] if r == 0 else shbuf[r, rows, :]
                    acc = acc + x * cw_ref[k:k + 1, sl]
        cbuf[:, sl] = acc

    c = cbuf[...]
    mu = jnp.mean(c, axis=-1, keepdims=True)
    cc = c - mu
    var = jnp.mean(cc * cc, axis=-1, keepdims=True)
    y = cc * lax.rsqrt(var + NORM_EPS) * lnw_ref[...] + lnb_ref[...]
    z = az_ref[...]
    out = (y * jax.nn.sigmoid(y)) * (z * jax.nn.sigmoid(z))
    c_ref[...] = out.astype(c_ref.dtype)

    sts_ref[0] = ubuf[hr + t - nh:hr + t, :]

    @pl.when(g == n_pc - 1)
    def _():
        stp_ref[0] = ubuf[hr + t - nh:hr + t, :]


def _conv_a(proj_a, l, n_pc, n_sc, hist, cw, cb, lnw, lnb):
    t = CHUNK
    nh = CONV_A_WIDTH - 1
    n_chunks = n_pc + n_sc
    w2 = C_A
    stream_of = lambda i: jnp.maximum(i - n_pc, 0)
    kern = functools.partial(_conv_a_kernel, n_pc=n_pc)
    return pl.pallas_call(
        kern,
        grid=(n_chunks,),
        in_specs=[pl.BlockSpec((t, w2), lambda i: (i, COL_AVAL // w2)),
                  pl.BlockSpec((t, w2), lambda i: (i, COL_AGLU // w2)),
                  pl.BlockSpec((t, w2), lambda i: (i, COL_AZ // w2)),
                  pl.BlockSpec((1, nh, C_A), lambda i: (l * n_sc + stream_of(i), 0, 0)),
                  pl.BlockSpec((None, CONV_A_WIDTH, C_A), lambda i: (l, 0, 0)),
                  _row_spec(C_A),
                  _row_spec(C_A),
                  _row_spec(C_A)],
        out_specs=[pl.BlockSpec((t, C_A), lambda i: (i, 0)),
                   pl.BlockSpec((1, nh, C_A), lambda i: (0, 0, 0)),
                   pl.BlockSpec((1, nh, C_A), lambda i: (stream_of(i), 0, 0))],
        out_shape=[jax.ShapeDtypeStruct((n_chunks * t, C_A), BF16),
                   jax.ShapeDtypeStruct((1, nh, C_A), F32),
                   jax.ShapeDtypeStruct((n_sc, nh, C_A), F32)],
        scratch_shapes=[pltpu.VMEM((A_HIST_ROWS + t, C_A), F32),
                        pltpu.VMEM((t, C_A), F32),
                        pltpu.VMEM((8, A_HIST_ROWS + t, CONV_A_STRIP), F32)],
        compiler_params=_cparams(("arbitrary",)),
        name="conv_a",
    )(proj_a, proj_a, proj_a, hist, cw, cb, lnw, lnb)


def _split_bf16(x, parts):
    out = []
    r = x
    for p in range(parts):
        hi = r.astype(BF16)
        out.append(hi)
        if p + 1 < parts:
            r = r - hi.astype(F32)
    return out


def _softplus(x):
    return jnp.maximum(x, 0.0) + jnp.log1p(jnp.exp(-jnp.abs(x)))


def _dt_chain_kernel(h_ref, wdt_ref, dtb_ref, alog_ref, ap_ref, dp_ref, tri_s):
    tm = h_ref.shape[0]
    dt_raw3 = lax.dot_general(h_ref[...], wdt_ref[...], (((1,), (1,)), ((), ())),
                              preferred_element_type=F32)
    dt3 = _softplus(dt_raw3 + dtb_ref[...])
    a3 = dt3 * (-jnp.exp(alog_ref[...]))
    @pl.when(pl.program_id(0) == 0)
    def _():
        ri = lax.broadcasted_iota(jnp.int32, (tm, tm), 0)
        ci = lax.broadcasted_iota(jnp.int32, (tm, tm), 1)
        chunk_bits = CHUNK.bit_length() - 1
        assert 1 << chunk_bits == CHUNK
        same_chunk = (ri >> chunk_bits) == (ci >> chunk_bits)
        tri_s[...] = jnp.where(same_chunk & (ci <= ri), 1.0, 0.0).astype(BF16)

    tri = tri_s[...]
    acum3 = jnp.zeros((tm, 3 * N_HEADS), F32)
    for part in _split_bf16(a3, 3):
        acum3 = acum3 + jnp.dot(tri, part, preferred_element_type=F32)
    lane3 = lax.broadcasted_iota(jnp.int32, (tm, 3 * N_HEADS), 1)
    a_hi, a_mid, a_lo = [p.astype(F32) for p in _split_bf16(acum3, 3)]
    ap_ref[...] = jnp.where(lane3 < N_HEADS, a_hi,
                            jnp.where(lane3 < 2 * N_HEADS, a_mid, a_lo)).astype(BF16)
    lane2 = lax.broadcasted_iota(jnp.int32, (tm, 2 * N_HEADS), 1)
    d_hi, d_lo = [p.astype(F32) for p in _split_bf16(dt3[:, 0:2 * N_HEADS], 2)]
    dp_ref[...] = jnp.where(lane2 < N_HEADS, d_hi, d_lo).astype(BF16)


def _dt_chain(h, wdt, l, dtb, alog, tm=512):
    n, k = h.shape
    return pl.pallas_call(
        _dt_chain_kernel,
        grid=(n // tm,),
        in_specs=[pl.BlockSpec((tm, k), lambda i: (i, 0)),
                  pl.BlockSpec((None, 3 * N_HEADS, k), lambda i: (l, 0, 0)),
                  _row_spec(3 * N_HEADS),
                  _row_spec(3 * N_HEADS)],
        out_specs=[pl.BlockSpec((tm, 3 * N_HEADS), lambda i: (i, 0)),
                   pl.BlockSpec((tm, 2 * N_HEADS), lambda i: (i, 0))],
        out_shape=[jax.ShapeDtypeStruct((n, 3 * N_HEADS), BF16),
                   jax.ShapeDtypeStruct((n, 2 * N_HEADS), BF16)],
        scratch_shapes=[pltpu.VMEM((tm, tm), BF16)],
        compiler_params=_cparams(("arbitrary",)),
        name="dt_chain",
    )(h, wdt, dtb, alog)


def _ssd_kernel(sz_ref, xs_ref, bc_ref, ap_s, dp_s, hist_ref, st0_ref, cw_ref, cb_ref,
                dskip_ref, gw_ref,
                yg_ref, cstp_ref, csts_ref, stp_ref, sts_ref,
                xbuf, xs_s, bc_s, st, e_s, ygbuf, *, n_pc):
    i = pl.program_id(0)
    t = CHUNK
    hr = S_HIST_ROWS
    nh = SSM_CONV_WIDTH - 1
    tb = 128

    @pl.when(i == 0)
    def _():
        hh = lax.broadcasted_iota(jnp.int32, (3 * N_HEADS, D_INNER), 0) & (N_HEADS - 1)
        jj = lax.broadcasted_iota(jnp.int32, (3 * N_HEADS, D_INNER), 1)
        e_s[...] = jnp.where((jj >> 6) == hh, 1.0, 0.0).astype(BF16)
        xbuf[0:hr, :] = jnp.zeros((hr, CONV_DIM), F32)
        st[...] = jnp.zeros((D_STATE, D_INNER), F32)

    @pl.when((i > 0) & (i < n_pc))
    def _():
        xbuf[0:hr, :] = xbuf[t:t + hr, :]

    @pl.when(i >= n_pc)
    def _():
        xbuf[hr - nh:hr, :] = hist_ref[0]
        for b in range(D_INNER // tb):
            st[:, b * tb:(b + 1) * tb] = st0_ref[0, b * tb:(b + 1) * tb, :].T

    xbuf[hr:hr + t, 0:D_INNER] = xs_ref[...]
    xbuf[hr:hr + t, D_INNER:CONV_DIM] = bc_ref[...]

    strip = 512

    def conv_strip(s):
        sl = slice(s * strip, (s + 1) * strip)
        win = xbuf[:, sl]
        part = win * cw_ref[0:1, sl]
        for k in range(1, SSM_CONV_WIDTH):
            part = pltpu.roll(part, 1, axis=0) + win * cw_ref[k:k + 1, sl]
        acc = part[hr:hr + t] + cb_ref[:, sl]
        acc = acc * jax.nn.sigmoid(acc)
        if s * strip < D_INNER:
            xs_s[:, sl] = acc
        else:
            bc_s[:, s * strip - D_INNER:(s + 1) * strip - D_INNER] = acc

    for s in range(CONV_DIM // strip):
        conv_strip(s)

    li = lax.broadcasted_iota(jnp.int32, (t, PACK), 0)
    ji = lax.broadcasted_iota(jnp.int32, (t, PACK), 1)
    si = ji & (HEAD_DIM - 1)
    hi4 = ji >> 6
    diag_mask = li == si
    causal_mask = li >= si

    def group_body(g):
        goff = g * GROUP_LANES
        boff = g * D_STATE
        bm = bc_s[:, pl.ds(boff, D_STATE)].astype(BF16)
        cm = bc_s[:, pl.ds(N_GROUPS * D_STATE + boff, D_STATE)].astype(BF16)
        bm4 = jnp.concatenate([bm, bm, bm, bm], axis=0)
        cb_rep = lax.dot_general(cm, bm4, (((1,), (1,)), ((), ())),
                                 preferred_element_type=F32)
        ss = jnp.zeros((t, 1), F32)
        for half in range(GROUP_LANES // PACK):
            lanes = pl.ds(goff + half * PACK, PACK)
            yoff = jnp.dot(cm, st[:, lanes].astype(BF16), preferred_element_type=F32)
            acol = jnp.dot(ap_s[...], e_s[:, lanes], preferred_element_type=F32)
            dtx = jnp.dot(dp_s[...], e_s[0:2 * N_HEADS, lanes], preferred_element_type=F32)
            xs = xs_s[:, lanes]
            xd = xs * dtx
            arow = jnp.sum(jnp.where(diag_mask, acol, 0.0), axis=0, keepdims=True)
            lmat = jnp.exp(jnp.where(causal_mask, acol - arow, -1e30))
            gmat = (cb_rep * lmat).astype(BF16)
            bd = jnp.concatenate(
                [jnp.where(hi4 == rr, xd, 0.0).astype(BF16) for rr in range(PACK // HEAD_DIM)],
                axis=0)
            ydiag = jnp.dot(gmat, bd, preferred_element_type=F32)
            y = ydiag + jnp.exp(acol) * yoff + dskip_ref[:, lanes] * xs
            z = sz_ref[:, lanes]
            yg = y * (z * jax.nn.sigmoid(z))
            ygbuf[:, lanes] = yg
            ss = ss + jnp.sum(yg * yg, axis=-1, keepdims=True)
            alast = acol[t - 1:t, :]
            xdd = (xd * jnp.exp(alast - acol)).astype(BF16)
            upd = lax.dot_general(bm, xdd, (((0,), (0,)), ((), ())),
                                  preferred_element_type=F32)
            st[:, lanes] = st[:, lanes] * jnp.exp(alast) + upd
        scale = lax.rsqrt(ss * (1.0 / GROUP_LANES) + NORM_EPS)
        glanes = pl.ds(goff, GROUP_LANES)
        yg_ref[:, glanes] = (ygbuf[:, glanes] * scale * gw_ref[:, glanes]).astype(yg_ref.dtype)

    for g in range(N_GROUPS):
        group_body(g)

    def write_state(cst_ref, stout_ref):
        cst_ref[0] = xbuf[hr + t - nh:hr + t, :]
        for b in range(D_INNER // tb):
            stout_ref[0, b * tb:(b + 1) * tb, :] = st[:, b * tb:(b + 1) * tb].T

    @pl.when(i == n_pc - 1)
    def _():
        write_state(cstp_ref, stp_ref)

    @pl.when(i >= n_pc)
    def _():
        write_state(csts_ref, sts_ref)


def _ssd(proj, ap, dp, l, n_pc, n_sc, hist, st0, cw, cb, dskip, gw):
    t = CHUNK
    nh = SSM_CONV_WIDTH - 1
    n_chunks = n_pc + n_sc
    seq_of = lambda i: (jnp.maximum(i - n_pc, 0), 0, 0)
    seq_of_l = lambda i: (l * n_sc + jnp.maximum(i - n_pc, 0), 0, 0)
    first = lambda i: (0, 0, 0)
    kern = functools.partial(_ssd_kernel, n_pc=n_pc)
    return pl.pallas_call(
        kern,
        grid=(n_chunks,),
        in_specs=[pl.BlockSpec((t, D_INNER), lambda i: (i, COL_SZ // D_INNER)),
                  pl.BlockSpec((t, D_INNER), lambda i: (i, COL_XS // D_INNER)),
                  pl.BlockSpec((t, 2048), lambda i: (i, COL_BC // 2048)),
                  pl.BlockSpec((t, 3 * N_HEADS), lambda i: (i, 0)),
                  pl.BlockSpec((t, 2 * N_HEADS), lambda i: (i, 0)),
                  pl.BlockSpec((1, nh, CONV_DIM), seq_of_l),
                  pl.BlockSpec((1, D_INNER, D_STATE), seq_of_l),
                  pl.BlockSpec((None, SSM_CONV_WIDTH, CONV_DIM), lambda i: (l, 0, 0)),
                  _row_spec(CONV_DIM),
                  _row_spec(D_INNER),
                  _row_spec(D_INNER)],
        out_specs=[pl.BlockSpec((t, D_INNER), lambda i: (i, 0)),
                   pl.BlockSpec((1, nh, CONV_DIM), first),
                   pl.BlockSpec((1, nh, CONV_DIM), seq_of),
                   pl.BlockSpec((1, D_INNER, D_STATE), first),
                   pl.BlockSpec((1, D_INNER, D_STATE), seq_of)],
        out_shape=[jax.ShapeDtypeStruct((n_chunks * t, D_INNER), BF16),
                   jax.ShapeDtypeStruct((1, nh, CONV_DIM), F32),
                   jax.ShapeDtypeStruct((n_sc, nh, CONV_DIM), F32),
                   jax.ShapeDtypeStruct((1, D_INNER, D_STATE), F32),
                   jax.ShapeDtypeStruct((n_sc, D_INNER, D_STATE), F32)],
        scratch_shapes=[pltpu.VMEM((S_HIST_ROWS + t, CONV_DIM), F32),
                        pltpu.VMEM((t, D_INNER), F32),
                        pltpu.VMEM((t, 2 * N_GROUPS * D_STATE), F32),
                        pltpu.VMEM((D_STATE, D_INNER), F32),
                        pltpu.VMEM((3 * N_HEADS, D_INNER), BF16),
                        pltpu.VMEM((t, D_INNER), F32)],
        compiler_params=_cparams(("arbitrary",)),
        name="ssd",
    )(proj, proj, proj, ap, dp, hist, st0, cw, cb, dskip, gw)


def _merge_kernel(c_ref, yg_ref, ga_ref, gb_ref, bga_ref, bgb_ref, wa_ref, wb_ref, m_ref):
    a_out = jnp.dot(c_ref[...], wa_ref[...], preferred_element_type=F32)
    b_out = jnp.dot(yg_ref[...], wb_ref[...], preferred_element_type=F32)
    ga = jax.nn.sigmoid(ga_ref[...] + bga_ref[...])
    gb = jax.nn.sigmoid(gb_ref[...] + bgb_ref[...])
    m_ref[...] = (ga * a_out + gb * b_out).astype(m_ref.dtype)


def _merge(c, yg, gates, l, bg3, wa, wb, tm=512, tn=1024):
    n = c.shape[0]
    nj = D_MODEL // tn
    return pl.pallas_call(
        _merge_kernel,
        grid=(nj, n // tm),
        in_specs=[pl.BlockSpec((tm, C_A), lambda j, i: (i, 0)),
                  pl.BlockSpec((tm, D_INNER), lambda j, i: (i, 0)),
                  pl.BlockSpec((tm, tn), lambda j, i: (i, j)),
                  pl.BlockSpec((tm, tn), lambda j, i: (i, nj + j)),
                  pl.BlockSpec((1, tn), lambda j, i: (0, j)),
                  pl.BlockSpec((1, tn), lambda j, i: (0, nj + j)),
                  pl.BlockSpec((None, C_A, tn), lambda j, i: (l, 0, j)),
                  pl.BlockSpec((None, D_INNER, tn), lambda j, i: (l, 0, j))],
        out_specs=pl.BlockSpec((tm, tn), lambda j, i: (i, j)),
        out_shape=jax.ShapeDtypeStruct((n, D_MODEL), BF16),
        compiler_params=_cparams(("parallel", "parallel")),
        name="merge",
    )(c, yg, gates, gates, bg3, bg3, wa, wb)


def _out_final_kernel(m_ref, w_ref, x_ref, nw_ref, yp_ref, ys_ref, *, n_pt):
    i = pl.program_id(0)
    x = x_ref[...] + jnp.dot(m_ref[...], w_ref[...], preferred_element_type=F32)
    ms = jnp.mean(x * x, axis=-1, keepdims=True)
    y = x * lax.rsqrt(ms + NORM_EPS) * nw_ref[...]

    @pl.when(i < n_pt)
    def _():
        yp_ref[...] = y

    @pl.when(i >= n_pt)
    def _():
        ys_ref[...] = y


def _out_final(m, w, l, x, final_w, n_p, tm=512):
    n = m.shape[0]
    tm = math.gcd(math.gcd(n_p, n - n_p), tm)
    n_pt = n_p // tm
    tile = pl.BlockSpec((tm, D_MODEL), lambda i: (i, 0))
    return pl.pallas_call(
        functools.partial(_out_final_kernel, n_pt=n_pt),
        grid=(n // tm,),
        in_specs=[tile, pl.BlockSpec((None, D_MODEL, D_MODEL), lambda i: (l, 0, 0)), tile,
                  _row_spec(D_MODEL)],
        out_specs=[pl.BlockSpec((tm, D_MODEL), lambda i: (jnp.minimum(i, n_pt - 1), 0)),
                   pl.BlockSpec((tm, D_MODEL), lambda i: (jnp.maximum(i - n_pt, 0), 0))],
        out_shape=[jax.ShapeDtypeStruct((n_p, D_MODEL), F32),
                   jax.ShapeDtypeStruct((n - n_p, D_MODEL), F32)],
        compiler_params=_cparams(("arbitrary",)),
        name="out_proj_final",
    )(m, w, x, final_w)


def _out_norm_kernel(m_ref, w_ref, x_ref, nw_ref, o_ref, h_ref):
    x = x_ref[...] + jnp.dot(m_ref[...], w_ref[...], preferred_element_type=F32)
    o_ref[...] = x
    ms = jnp.mean(x * x, axis=-1, keepdims=True)
    h_ref[...] = (x * lax.rsqrt(ms + NORM_EPS) * nw_ref[...]).astype(h_ref.dtype)


def _out_proj(m, w, l, x, next_norm_w, tm=512):
    n = m.shape[0]
    tile = pl.BlockSpec((tm, D_MODEL), lambda i: (i, 0))
    w_spec = pl.BlockSpec((None, D_MODEL, D_MODEL), lambda i: (l, 0, 0))
    x_shape = jax.ShapeDtypeStruct((n, D_MODEL), F32)
    return pl.pallas_call(
        _out_norm_kernel, grid=(n // tm,),
        in_specs=[tile, w_spec, tile, _row_spec(D_MODEL)], out_specs=[tile, tile],
        out_shape=[x_shape, jax.ShapeDtypeStruct((n, D_MODEL), BF16)],
        compiler_params=_cparams(("parallel",)), name="out_proj_norm",
    )(m, w, x, next_norm_w)


def kernel(x_prompt, x_sample, state_conv_a, state_conv_s, state_ssm, norm_w, w_in, b_gate,
           conv_a_w, conv_a_b, ln_a_w, ln_a_b, w_a_out, conv_s_w, conv_s_b, dt_bias, a_log,
           d_skip, gnorm_w, w_b_out, w_out, final_norm_w):
    depth = w_in.shape[0]
    batch, seq, _ = x_prompt.shape
    dec_batch, dec_seq, _ = x_sample.shape
    assert batch == 1 and dec_seq == CHUNK and seq % CHUNK == 0
    assert w_in.shape[2] == W_COL_GATE + 2 * D_MODEL
    n_p = batch * seq
    n_s = dec_batch * dec_seq
    n_pc = n_p // CHUNK

    w_in_t = jnp.swapaxes(w_in, 1, 2)
    w_dt_t = jnp.tile(w_in_t[:, W_COL_DT:W_COL_GATE, :], (1, 3, 1)).astype(BF16)
    dtb3 = jnp.tile(dt_bias, (1, 3))
    alog3 = jnp.tile(a_log, (1, 3))
    dskip = jnp.repeat(d_skip, HEAD_DIM, axis=1)
    st0 = state_ssm.reshape(depth * dec_batch, D_INNER, D_STATE)
    hist_a = state_conv_a.reshape(depth * dec_batch, CONV_A_WIDTH - 1, C_A)
    hist_s = state_conv_s.reshape(depth * dec_batch, SSM_CONV_WIDTH - 1, CONV_DIM)

    wa_bf, wb_bf, wo_bf = w_a_out.astype(BF16), w_b_out.astype(BF16), w_out.astype(BF16)

    conv_a_p, conv_s_p, ssm_p, conv_a_s, conv_s_s, ssm_s = [], [], [], [], [], []
    x, h = _first_norm(x_prompt.reshape(n_p, D_MODEL), x_sample.reshape(n_s, D_MODEL),
                       norm_w[0].reshape(1, -1))
    for l in range(depth):
        row = lambda v: v[l].reshape(1, -1)
        proj_a = _in_proj(h, w_in_t, l, 0, W_COL_SZ)
        proj_b = _in_proj(h, w_in_t, l, W_COL_SZ, W_COL_DT - W_COL_SZ)
        gates = _in_proj(h, w_in_t, l, W_COL_GATE, 2 * D_MODEL)
        ap, dp = _dt_chain(h, w_dt_t, l, row(dtb3), row(alog3))
        c, ca_p, ca_s = _conv_a(proj_a, l, n_pc, dec_batch, hist_a, conv_a_w,
                                row(conv_a_b), row(ln_a_w), row(ln_a_b))
        yg, cs_p, cs_s, st_p, st_s = _ssd(
            proj_b, ap, dp, l, n_pc, dec_batch, hist_s, st0, conv_s_w,
            row(conv_s_b), row(dskip), row(gnorm_w))
        m = _merge(c, yg, gates, l, row(b_gate), wa_bf, wb_bf)
        if l + 1 < depth:
            x, h = _out_proj(m, wo_bf, l, x, norm_w[l + 1].reshape(1, -1))
        else:
            y_p, y_s = _out_final(m, wo_bf, l, x, final_norm_w.reshape(1, D_MODEL), n_p)

        conv_a_p.append(ca_p)
        conv_a_s.append(ca_s)
        conv_s_p.append(cs_p)
        conv_s_s.append(cs_s)
        ssm_p.append(st_p.reshape(batch, N_HEADS, HEAD_DIM, D_STATE))
        ssm_s.append(st_s.reshape(dec_batch, N_HEADS, HEAD_DIM, D_STATE))

    y_prompt = y_p.reshape(batch, seq, D_MODEL)
    y_sample = y_s.reshape(dec_batch, dec_seq, D_MODEL)
    return (y_prompt, y_sample, jnp.stack(conv_a_p), jnp.stack(conv_s_p), jnp.stack(ssm_p),
            jnp.stack(conv_a_s), jnp.stack(conv_s_s), jnp.stack(ssm_s))
```
